```python
import math
import jax
import jax.numpy as jnp
from jax import lax
import numpy as np

D_MODEL = 1024
BATCH = 1
SEQ = 16384
DEPTH = 2
DEC_BATCH = 32
DEC_SEQ = 8
PAST_LEN = 16384
PAGE_SIZE = 128

GLA_WIDTH = D_MODEL // 2
GLA_HEADS = 4
GLA_DV = GLA_WIDTH // GLA_HEADS
GLA_DK = GLA_DV // 2
GLA_QK_WIDTH = GLA_HEADS * GLA_DK
GATE_RANK = 16
GATE_TAU = 16.0
GLA_CHUNK = 64
SB_WIDTH = D_MODEL - GLA_WIDTH
SB_HEAD_DIM = 64
SB_HEADS = SB_WIDTH // SB_HEAD_DIM
SB_BLOCK = 128
SB_BIAS_INIT = -8.0
MIX_WIDTH = GLA_WIDTH + SB_WIDTH
PROJ_WIDTHS = (GLA_QK_WIDTH, GLA_QK_WIDTH, GLA_WIDTH, GLA_WIDTH, GATE_RANK, SB_WIDTH, SB_WIDTH, SB_WIDTH)
PROJ_DIM = sum(PROJ_WIDTHS)
D_FF = 256 * ((8 * D_MODEL // 3 + 255) // 256)
N_EXPERTS = 8
TOP_K = 2
D_FF_EXPERT = 7 * D_MODEL // 2
N_DENSE = (DEPTH + 1) // 2
N_MOE = DEPTH // 2
EPS = 1e-6

kernel_name = 'hymba_gla_stickbreaking_adaln_decode_step'


def rmsnorm(x, g):
    xf = x.astype(jnp.float32)
    return xf * lax.rsqrt(jnp.mean(xf * xf, axis=-1, keepdims=True) + EPS) * g.astype(jnp.float32)


def head_rmsnorm(o, g):
    B, T, H, dh = o.shape
    return rmsnorm(o, g.reshape(H, dh)).reshape(B, T, H * dh)


def swiglu(h, w_up, w_down):
    gate, val = jnp.split(h @ w_up, 2, axis=-1)
    return (jax.nn.silu(gate) * val) @ w_down


def moe_swiglu(h, w_router, w_up, w_down):
    logits = (h @ w_router).astype(jnp.float32)
    top_val, top_idx = lax.top_k(logits, TOP_K)
    top_w = jax.nn.softmax(top_val, axis=-1)
    gates = jnp.sum(jax.nn.one_hot(top_idx, N_EXPERTS, dtype=jnp.float32) * top_w[..., None], axis=-2)
    y = jnp.zeros(h.shape[:-1] + (w_down.shape[-1],), jnp.float32)
    for e in range(N_EXPERTS):
        y = y + gates[..., e:e + 1] * swiglu(h, w_up[e], w_down[e])
    return y


def gla_chunked(q, k, v, log_a, s0):
    B, T, H, dk = q.shape
    dv = v.shape[-1]
    chunk = GLA_CHUNK if T % GLA_CHUNK == 0 else T
    n = T // chunk

    def to_chunks(t):
        return t.astype(jnp.float32).reshape(B, n, chunk, H, t.shape[-1]).transpose(1, 0, 2, 3, 4)

    causal = jnp.tril(jnp.ones((chunk, chunk), dtype=bool))[None, :, :, None, None]

    def step(S, inp):
        qi, ki, vi, gi = inp
        L = jnp.cumsum(gi, axis=1)
        o_inter = jnp.einsum('bthk,bhkv->bthv', qi * jnp.exp(L), S)
        decay = jnp.exp(jnp.where(causal, L[:, :, None] - L[:, None, :], -jnp.inf))
        scores = jnp.einsum('bthk,bshk,btshk->bhts', qi, ki, decay)
        o_intra = jnp.einsum('bhts,bshv->bthv', scores, vi)
        L_last = L[:, -1]
        S_new = jnp.exp(L_last)[..., None] * S + jnp.einsum('bshk,bshv->bhkv', ki * jnp.exp(L_last[:, None] - L), vi)
        return S_new, o_inter + o_intra

    S_final, o = lax.scan(step, s0.astype(jnp.float32), (to_chunks(q), to_chunks(k), to_chunks(v), to_chunks(log_a)))
    return o.transpose(1, 0, 2, 3, 4).reshape(B, T, H, dv), S_final


def stick_breaking(q, k, v, bias, q_pos, k_pos):
    B, Tq, H, d = q.shape
    block = SB_BLOCK if Tq % SB_BLOCK == 0 else Tq
    nb = Tq // block
    kf = k.astype(jnp.float32)
    vf = v.astype(jnp.float32)
    bf = bias.astype(jnp.float32)[None, :, None, None]
    qb = (q.astype(jnp.float32) * d ** -0.5).reshape(B, nb, block, H, d).transpose(1, 0, 2, 3, 4)
    pb = q_pos.reshape(nb, block)

    def one_block(args):
        qi, pi = args
        z = jnp.einsum('bqhd,bkhd->bhqk', qi, kf) + bf
        mask = k_pos[None, :] < pi[:, None]
        sp = jnp.where(mask, jax.nn.softplus(z), 0.0)
        later = lax.cumsum(sp, axis=3, reverse=True) - sp
        a = jnp.where(mask, jnp.exp(jax.nn.log_sigmoid(z) - later), 0.0)
        return jnp.einsum('bhqk,bkhd->bqhd', a, vf)

    o = lax.map(one_block, (qb, pb))
    return o.transpose(1, 0, 2, 3, 4).reshape(B, Tq, H, d)


def mixer(l, h, gla_s0, k_past, v_past, q_pos, k_pos, p):
    B, T, _ = h.shape
    split_points = [int(s) for s in np.cumsum(PROJ_WIDTHS)[:-1]]
    gq, gk, gv, gg, ga, sq, sk, sv = jnp.split(h @ p['w_in'][l], split_points, axis=-1)
    q = gq.reshape(B, T, GLA_HEADS, GLA_DK) * GLA_DK ** -0.5
    k = gk.reshape(B, T, GLA_HEADS, GLA_DK)
    v = gv.reshape(B, T, GLA_HEADS, GLA_DV)
    log_a = (jax.nn.log_sigmoid((ga @ p['w_gate_up'][l] + p['b_gate'][l]).astype(jnp.float32)) / GATE_TAU).reshape(B, T, GLA_HEADS, GLA_DK)
    o_gla, s_new = gla_chunked(q, k, v, log_a, gla_s0)
    o_gla = head_rmsnorm(o_gla, p['norm_gla'][l]) * jax.nn.silu(gg.astype(jnp.float32))
    sq = sq.reshape(B, T, SB_HEADS, SB_HEAD_DIM)
    sk = sk.reshape(B, T, SB_HEADS, SB_HEAD_DIM)
    sv = sv.reshape(B, T, SB_HEADS, SB_HEAD_DIM)
    if k_past is None:
        k_all, v_all = sk, sv
    else:
        k_all = jnp.concatenate([k_past.astype(sk.dtype), sk], axis=1)
        v_all = jnp.concatenate([v_past.astype(sv.dtype), sv], axis=1)
    o_sb = head_rmsnorm(stick_breaking(sq, k_all, v_all, p['sb_bias'][l], q_pos, k_pos), p['norm_sb'][l])
    mixed = jnp.concatenate([o_gla, o_sb], axis=-1).astype(h.dtype) @ p['w_out'][l]
    return mixed, s_new, sk, sv


def layer_forward(l, x, c, gla_s0, k_past, v_past, q_pos, k_pos, p):
    mod = (jax.nn.silu(c.astype(jnp.float32)) @ p['w_ada'][l].astype(jnp.float32) + p['b_ada'][l].astype(jnp.float32))[:, None, :]
    shift1, scale1, gate1, shift2, scale2, gate2 = jnp.split(mod, 6, axis=-1)
    h = (rmsnorm(x, p['norm_attn'][l]) * (1.0 + scale1) + shift1).astype(x.dtype)
    mixed, s_new, k_new, v_new = mixer(l, h, gla_s0, k_past, v_past, q_pos, k_pos, p)
    x = (x + gate1 * mixed).astype(x.dtype)
    h = (rmsnorm(x, p['norm_ffn'][l]) * (1.0 + scale2) + shift2).astype(x.dtype)
    if l % 2 == 0:
        ffn = swiglu(h, p['w_ff_up'][l // 2], p['w_ff_down'][l // 2])
    else:
        ffn = moe_swiglu(h, p['w_router'][l // 2], p['w_exp_up'][l // 2], p['w_exp_down'][l // 2])
    x = (x + gate2 * ffn).astype(x.dtype)
    return x, s_new, k_new, v_new


def setup_inputs(seed: int = 0) -> dict:
    key = jax.random.key(seed)
    ks = jax.random.split(key, 25)
    f32 = jnp.float32

    def nrm(k, shape, scale):
        return jax.random.normal(k, shape, f32) * scale

    D = D_MODEL
    n_pages = PAST_LEN // PAGE_SIZE
    n_used = DEC_BATCH * n_pages
    n_phys = n_used + n_used // 4
    page_table = jax.random.permutation(ks[0], n_phys)[:n_used].reshape(DEC_BATCH, n_pages).astype(jnp.int32)
    pool_shape = (DEPTH, n_phys, PAGE_SIZE, SB_HEADS, SB_HEAD_DIM)
    return {
        'x_prompt': nrm(ks[1], (BATCH, SEQ, D), 1.0),
        'x_sample': nrm(ks[2], (DEC_BATCH, DEC_SEQ, D), 1.0),
        'c_prompt': nrm(ks[3], (BATCH, D), 1.0),
        'c_sample': nrm(ks[4], (DEC_BATCH, D), 1.0),
        'cache_sb_k': nrm(ks[5], pool_shape, 1.0),
        'cache_sb_v': nrm(ks[6], pool_shape, 1.0),
        'state_gla': nrm(ks[7], (DEPTH, DEC_BATCH, GLA_HEADS, GLA_DK, GLA_DV), 1.0),
        'page_table': page_table,
        'w_ada': nrm(ks[8], (DEPTH, D, 6 * D), 0.5 * D ** -0.5),
        'b_ada': nrm(ks[9], (DEPTH, 6 * D), 0.02),
        'norm_attn': 1.0 + nrm(ks[10], (DEPTH, D), 0.02),
        'norm_ffn': 1.0 + nrm(ks[11], (DEPTH, D), 0.02),
        'w_in': nrm(ks[12], (DEPTH, D, PROJ_DIM), D ** -0.5),
        'w_gate_up': nrm(ks[13], (DEPTH, GATE_RANK, GLA_QK_WIDTH), GATE_RANK ** -0.5),
        'b_gate': nrm(ks[14], (DEPTH, GLA_QK_WIDTH), 0.1),
        'norm_gla': 1.0 + nrm(ks[15], (DEPTH, GLA_WIDTH), 0.02),
        'sb_bias': SB_BIAS_INIT + nrm(ks[24], (DEPTH, SB_HEADS), 0.5),
        'norm_sb': 1.0 + nrm(ks[16], (DEPTH, SB_WIDTH), 0.02),
        'w_out': nrm(ks[17], (DEPTH, MIX_WIDTH, D), MIX_WIDTH ** -0.5),
        'w_ff_up': nrm(ks[18], (N_DENSE, D, 2 * D_FF), D ** -0.5),
        'w_ff_down': nrm(ks[19], (N_DENSE, D_FF, D), D_FF ** -0.5),
        'w_router': nrm(ks[20], (N_MOE, D, N_EXPERTS), D ** -0.5),
        'w_exp_up': nrm(ks[21], (N_MOE, N_EXPERTS, D, 2 * D_FF_EXPERT), D ** -0.5),
        'w_exp_down': nrm(ks[22], (N_MOE, N_EXPERTS, D_FF_EXPERT, D), D_FF_EXPERT ** -0.5),
        'norm_final': 1.0 + nrm(ks[23], (D,), 0.02),
    }


def reference(x_prompt, x_sample, c_prompt, c_sample, cache_sb_k, cache_sb_v, state_gla, page_table,
              w_ada, b_ada, norm_attn, norm_ffn, w_in, w_gate_up, b_gate, norm_gla, sb_bias, norm_sb, w_out,
              w_ff_up, w_ff_down, w_router, w_exp_up, w_exp_down, norm_final):
    p = {'w_ada': w_ada, 'b_ada': b_ada, 'norm_attn': norm_attn, 'norm_ffn': norm_ffn, 'w_in': w_in,
         'w_gate_up': w_gate_up, 'b_gate': b_gate, 'norm_gla': norm_gla, 'sb_bias': sb_bias,
         'norm_sb': norm_sb, 'w_out': w_out, 'w_ff_up': w_ff_up, 'w_ff_down': w_ff_down,
         'w_router': w_router, 'w_exp_up': w_exp_up, 'w_exp_down': w_exp_down}
    B, T, _ = x_prompt.shape
    DB, TS, _ = x_sample.shape
    n_pages = page_table.shape[1]
    past_len = n_pages * cache_sb_k.shape[2]
    pos_prompt = jnp.arange(T, dtype=jnp.int32)
    q_pos_sample = past_len + jnp.arange(TS, dtype=jnp.int32)
    k_pos_sample = jnp.arange(past_len + TS, dtype=jnp.int32)
    gla_zero = jnp.zeros((B, GLA_HEADS, GLA_DK, GLA_DV), jnp.float32)

    xp, xs = x_prompt, x_sample
    kp_list, vp_list, sp_list, ks_list, vs_list, ss_list = [], [], [], [], [], []
    for l in range(DEPTH):
        xp, s_p, k_p, v_p = layer_forward(l, xp, c_prompt, gla_zero, None, None, pos_prompt, pos_prompt, p)
        k_past = cache_sb_k[l][page_table].reshape(DB, past_len, SB_HEADS, SB_HEAD_DIM)
        v_past = cache_sb_v[l][page_table].reshape(DB, past_len, SB_HEADS, SB_HEAD_DIM)
        xs, s_s, k_s, v_s = layer_forward(l, xs, c_sample, state_gla[l], k_past, v_past, q_pos_sample, k_pos_sample, p)
        kp_list.append(k_p)
        vp_list.append(v_p)
        sp_list.append(s_p)
        ks_list.append(k_s)
        vs_list.append(v_s)
        ss_list.append(s_s)

    y_prompt = rmsnorm(xp, norm_final).astype(x_prompt.dtype)
    y_sample = rmsnorm(xs, norm_final).astype(x_sample.dtype)
    return (y_prompt, y_sample, jnp.stack(kp_list), jnp.stack(vp_list), jnp.stack(sp_list),
            jnp.stack(ks_list), jnp.stack(vs_list), jnp.stack(ss_list))
```

```python
import functools

import jax
import jax.numpy as jnp
from jax import lax
from jax.experimental import pallas as pl
from jax.experimental.pallas import tpu as pltpu

F32 = jnp.float32
BF16 = jnp.bfloat16
EPS = 1e-6
GATE_TAU = 16.0
GLA_CHUNK = 64
GLA_SUB = 16
TOP_K = 2
LANES = 128
SB_BLOCK = 256
PAGES_PER_STEP = 8
ROW_TILE = 1024
VMEM_LIMIT = 56 * 1024 * 1024


def _params(sem):
    return pltpu.CompilerParams(dimension_semantics=sem, vmem_limit_bytes=VMEM_LIMIT)


def _dot(a, b):
    return jnp.dot(a, b, preferred_element_type=F32)


def _dot_nt(a, b):
    return lax.dot_general(a, b, (((1,), (1,)), ((), ())), preferred_element_type=F32)


def _dot_tn(a, b):
    return lax.dot_general(a, b, (((0,), (0,)), ((), ())), preferred_element_type=F32)


def _split(x):
    hi = x.astype(BF16)
    return hi, (x - hi.astype(F32)).astype(BF16)


def _dot_hp(a, b):
    ah, al = _split(a)
    bh, bl = _split(b)
    return _dot(ah, bh) + (_dot(ah, bl) + _dot(al, bh))


def _dot_hp_exact_rhs(a, b_bf16):
    ah, al = _split(a)
    return _dot(ah, b_bf16) + _dot(al, b_bf16)


def _silu(x):
    return x * (1.0 / (1.0 + jnp.exp(-x)))


def _log_sigmoid(x):
    return jnp.minimum(x, 0.0) - jnp.log1p(jnp.exp(-jnp.abs(x)))


def _softplus(x):
    return jnp.maximum(x, 0.0) + jnp.log(1.0 + jnp.exp(-jnp.abs(x)))


def _idiv(x, n):
    if n & (n - 1) == 0:
        return lax.shift_right_logical(x, n.bit_length() - 1)
    return x // n


def _iota(shape, dim):
    return lax.broadcasted_iota(jnp.int32, shape, dim)


def _modulated_norm(x, g, scale, shift):
    ms = jnp.mean(x * x, axis=-1, keepdims=True)
    return (x * lax.rsqrt(ms + EPS) * g) * (1.0 + scale) + shift


def _row_tiling(b, t, target):
    if t >= target:
        assert t % target == 0
        return 1, target
    tb = min(b, max(1, target // t))
    assert b % tb == 0
    return tb, t


def _mod_kernel(c_ref, w_ref, b_ref, o_ref):
    o_ref[0] = _dot_hp(_silu(c_ref[...]), w_ref[0]) + b_ref[0]


def _modulation(c, w_ada, b_ada):
    depth, d, n = w_ada.shape
    bc = c.shape[0]
    tn = n // 4
    return pl.pallas_call(
        _mod_kernel,
        grid=(depth, n // tn),
        in_specs=[pl.BlockSpec((bc, d), lambda l, j: (0, 0)),
                  pl.BlockSpec((1, d, tn), lambda l, j: (l, 0, j)),
                  pl.BlockSpec((1, 1, tn), lambda l, j: (l, 0, j))],
        out_specs=pl.BlockSpec((1, bc, tn), lambda l, j: (l, 0, j)),
        out_shape=jax.ShapeDtypeStruct((depth, bc, n), F32),
        compiler_params=_params(("parallel", "parallel")),
        name="adaln_modulation",
    )(c, w_ada, b_ada.reshape(depth, 1, n))


def _inproj_kernel(x_ref, shift_ref, scale_ref, g_ref, w_ref, wg_ref, bg_ref,
                   gq_ref, gk_ref, gv_ref, gg_ref, la_ref, sq_ref, sk_ref, sv_ref, skb_ref, svb_ref,
                   *, qk, gw, sw, q_scale_gla, q_scale_sb):
    tb, ts, d = x_ref.shape
    h = _modulated_norm(x_ref[...], g_ref[...], scale_ref[...], shift_ref[...])
    hb = h.reshape(tb * ts, d).astype(BF16)
    off = [0]

    def proj(width):
        a = off[0]
        off[0] = a + width
        return _dot(hb, w_ref[:, a:a + width])

    gq_ref[...] = proj(qk) * q_scale_gla
    gk_ref[...] = proj(qk)
    gv_ref[...] = proj(gw)
    gg_ref[...] = proj(gw)
    sq_ref[...] = (proj(sw) * q_scale_sb).astype(BF16)
    sk = proj(sw)
    sk_ref[...] = sk
    skb_ref[...] = sk.astype(BF16)
    sv = proj(sw)
    sv_ref[...] = sv
    svb_ref[...] = sv.astype(BF16)
    ga = proj(LANES)
    gate = _dot_hp(ga, wg_ref[...]) + bg_ref[...]
    la_ref[...] = _log_sigmoid(gate) * (1.0 / GATE_TAU)


def _input_projection(x, shift, scale, g, w, wg, bg, *, qk, gw, sw, dk, hd):
    b, t, d = x.shape
    n = b * t
    tb, ts = _row_tiling(b, t, ROW_TILE // 2)
    tm = tb * ts
    nt = t // ts
    x_spec = pl.BlockSpec((tb, ts, d), lambda i, j: (i, j, 0))
    m_spec = pl.BlockSpec((tb, 1, d), lambda i, j: (i, 0, 0))

    def full(a):
        return pl.BlockSpec(a.shape, lambda i, j: (0,) * a.ndim)

    def out(width, dtype):
        return (pl.BlockSpec((tm, width), lambda i, j: (i * nt + j, 0)),
                jax.ShapeDtypeStruct((n, width), dtype))

    outs = [out(qk, F32), out(qk, F32), out(gw, F32), out(gw, F32), out(qk, F32),
            out(sw, BF16), out(sw, F32), out(sw, F32), out(sw, BF16), out(sw, BF16)]
    kern = functools.partial(_inproj_kernel, qk=qk, gw=gw, sw=sw,
                             q_scale_gla=dk ** -0.5, q_scale_sb=hd ** -0.5)
    return pl.pallas_call(
        kern,
        grid=(b // tb, nt),
        in_specs=[x_spec, m_spec, m_spec, full(g), full(w), full(wg), full(bg)],
        out_specs=[o[0] for o in outs],
        out_shape=[o[1] for o in outs],
        compiler_params=_params(("parallel", "parallel")),
        name="norm_input_projection",
    )(x, shift, scale, g, w, wg, bg)


def _gla_kernel(q_ref, k_ref, v_ref, la_ref, gg_ref, s0_ref, g_ref, o_ref, s_out_ref, st_ref,
                *, chunk, sub):
    _, nh, dk, dv = s0_ref.shape
    tblock = q_ref.shape[1]
    qk, vw = nh * dk, nh * dv
    t = pl.program_id(1)

    @pl.when(t == 0)
    def _():
        rows = []
        for h in range(nh):
            rows.append(jnp.concatenate(
                [s0_ref[0, h] if hh == h else jnp.zeros((dk, dv), F32) for hh in range(nh)], axis=1))
        st_ref[...] = jnp.concatenate(rows, axis=0).T

    state_mask = _idiv(_iota((vw, qk), 0), dv) == _idiv(_iota((vw, qk), 1), dk)
    head_expand = (_idiv(_iota((qk, vw), 0), dk) == _idiv(_iota((qk, vw), 1), dv)).astype(BF16)
    key_mask = _idiv(_iota((nh * sub, qk), 0), sub) == _idiv(_iota((nh * sub, qk), 1), dk)
    val_mask = _idiv(_iota((nh * sub, vw), 0), sub) == _idiv(_iota((nh * sub, vw), 1), dv)
    tril = (_iota((chunk, chunk), 0) >= _iota((chunk, chunk), 1)).astype(BF16)
    row_id = _iota((chunk, 1), 0)
    pair_mask = _iota((sub, sub, 1), 0) >= _iota((sub, sub, 1), 1)
    n_sub = chunk // sub

    def one_chunk(ci, carry):
        r = pl.multiple_of(ci * chunk, chunk)
        q = q_ref[0, pl.ds(r, chunk), :]
        k = k_ref[0, pl.ds(r, chunk), :]
        v = v_ref[0, pl.ds(r, chunk), :]
        la_hi, la_lo = _split(la_ref[0, pl.ds(r, chunk), :])
        cum = _dot(tril, la_hi) + _dot(tril, la_lo)
        last = cum[chunk - 1:chunk, :]
        st = st_ref[...]

        o = _dot_nt((q * jnp.exp(cum)).astype(BF16), st.astype(BF16))
        k_dec = (k * jnp.exp(last - cum)).astype(BF16)
        upd = _dot_tn(v.astype(BF16), k_dec)
        st_ref[...] = st * jnp.exp(last) + jnp.where(state_mask, upd, 0.0)

        for j in range(n_sub - 1):
            a = j * sub
            ref_row = cum[a + sub - 1:a + sub, :]
            q_dec = (q * jnp.exp(jnp.minimum(cum - ref_row, 0.0))).astype(BF16)
            k_j = k[a:a + sub] * jnp.exp(ref_row - cum[a:a + sub])
            k_bd = jnp.where(key_mask, jnp.concatenate([k_j] * nh, axis=0), 0.0).astype(BF16)
            sc = _dot_nt(q_dec, k_bd)
            sc = jnp.where(row_id >= a + sub, sc, 0.0).astype(BF16)
            v_bd = jnp.where(val_mask, jnp.concatenate([v[a:a + sub]] * nh, axis=0), 0.0).astype(BF16)
            o = o + _dot(sc, v_bd)

        diag = []
        for j in range(n_sub):
            a = j * sub
            l_j, q_j, k_j, v_j = cum[a:a + sub], q[a:a + sub], k[a:a + sub], v[a:a + sub]
            dec = jnp.exp(jnp.minimum(l_j[:, None, :] - l_j[None, :, :], 0.0))
            p = jnp.where(pair_mask, (q_j[:, None, :] * k_j[None, :, :]) * dec, 0.0)
            sc = _dot(p.reshape(sub * sub, qk).astype(BF16), head_expand)
            diag.append(jnp.sum(sc.reshape(sub, sub, vw) * v_j[None, :, :], axis=1))
        o = o + (diag[0] if n_sub == 1 else jnp.concatenate(diag, axis=0))

        g = g_ref[...]
        gg = gg_ref[0, pl.ds(r, chunk), :]
        outs = []
        for h in range(nh):
            oh = o[:, h * dv:(h + 1) * dv]
            ms = jnp.mean(oh * oh, axis=-1, keepdims=True)
            outs.append(oh * lax.rsqrt(ms + EPS) * g[:, h * dv:(h + 1) * dv])
        o_ref[0, pl.ds(r, chunk), :] = jnp.concatenate(outs, axis=1) * _silu(gg)
        return carry

    lax.fori_loop(0, tblock // chunk, one_chunk, 0)

    @pl.when(t == pl.num_programs(1) - 1)
    def _():
        s_bd = st_ref[...].T
        for h in range(nh):
            s_out_ref[0, h] = s_bd[h * dk:(h + 1) * dk, h * dv:(h + 1) * dv]


def _gla(q, k, v, la, gg, s0, g):
    b, t, qk = q.shape
    vw = v.shape[-1]
    chunk = GLA_CHUNK if t % GLA_CHUNK == 0 else t
    sub = GLA_SUB if chunk % GLA_SUB == 0 else chunk
    tblock = min(t, 8 * chunk)
    assert t % tblock == 0

    def tok(width):
        return pl.BlockSpec((1, tblock, width), lambda i, j: (i, j, 0))

    s_spec = pl.BlockSpec((1,) + s0.shape[1:], lambda i, j: (i, 0, 0, 0))
    return pl.pallas_call(
        functools.partial(_gla_kernel, chunk=chunk, sub=sub),
        grid=(b, t // tblock),
        in_specs=[tok(qk), tok(qk), tok(vw), tok(qk), tok(vw), s_spec,
                  pl.BlockSpec(g.shape, lambda i, j: (0, 0))],
        out_specs=[tok(vw), s_spec],
        out_shape=[jax.ShapeDtypeStruct((b, t, vw), F32), jax.ShapeDtypeStruct(s0.shape, F32)],
        scratch_shapes=[pltpu.VMEM((vw, qk), F32)],
        compiler_params=_params(("parallel", "arbitrary")),
        name="gla_chunked",
    )(q, k, v, la, gg, s0, g)


def _sb_block(z, tri, run, valid):
    sp = _softplus(z)
    if valid is not None:
        sp = jnp.where(valid, sp, 0.0)
    within = _dot(sp.astype(BF16), tri)
    a = jnp.exp(z - (within + run))
    if valid is not None:
        a = jnp.where(valid, a, 0.0)
    return a.astype(BF16), run + within[:, 0:1]


def _sb_prompt_kernel(bias_ref, q_ref, k_ref, v_ref, g_ref, o_ref, acc_ref, run_ref, *, hd):
    blk = q_ref.shape[0]
    p = pl.program_id(1)
    i = pl.program_id(2)
    q = q_ref[...]
    first = _iota((1, LANES), 1) < hd
    zero = jnp.zeros_like(q)
    q_heads = (jnp.where(first, q, zero), jnp.where(first, zero, q))
    biases = (bias_ref[2 * p], bias_ref[2 * p + 1])
    row = _iota((blk, blk), 0)
    col = _iota((blk, blk), 1)
    tri = (row >= col).astype(BF16)
    causal = col < row
    acc_ref[...] = jnp.zeros_like(acc_ref)
    run_ref[...] = jnp.zeros_like(run_ref)

    def block(jb, valid):
        off = pl.multiple_of(jb * blk, blk)
        kb = k_ref[pl.ds(off, blk), :]
        vb = v_ref[pl.ds(off, blk), :]
        for h in range(2):
            z = _dot_nt(q_heads[h], kb) + biases[h]
            a, run = _sb_block(z, tri, run_ref[h], valid)
            run_ref[h] = run
            acc_ref[h] += _dot(a, vb)

    block(i, causal)

    def body(j, carry):
        block(i - j, None)
        return carry

    lax.fori_loop(1, i + 1, body, 0)

    o = jnp.where(first, acc_ref[0], acc_ref[1])
    ss = o * o
    s0 = jnp.sum(jnp.where(first, ss, 0.0), axis=-1, keepdims=True)
    s1 = jnp.sum(jnp.where(first, 0.0, ss), axis=-1, keepdims=True)
    ms = jnp.where(first, s0, s1) * (1.0 / hd)
    o_ref[...] = (o * lax.rsqrt(ms + EPS) * g_ref[...]).astype(BF16)


def _sb_prompt(q, k, v, bias, g, *, hd):
    b, t, w = q.shape
    blk = SB_BLOCK if t % SB_BLOCK == 0 else t
    n_pairs = w // LANES
    return pl.pallas_call(
        functools.partial(_sb_prompt_kernel, hd=hd),
        grid=(b, n_pairs, t // blk),
        in_specs=[pl.BlockSpec(memory_space=pltpu.SMEM),
                  pl.BlockSpec((None, blk, LANES), lambda bi, p, i: (bi, i, p)),
                  pl.BlockSpec((None, t, LANES), lambda bi, p, i: (bi, 0, p)),
                  pl.BlockSpec((None, t, LANES), lambda bi, p, i: (bi, 0, p)),
                  pl.BlockSpec((1, LANES), lambda bi, p, i: (0, p))],
        out_specs=pl.BlockSpec((None, blk, LANES), lambda bi, p, i: (bi, i, p)),
        out_shape=jax.ShapeDtypeStruct((b, t, w), BF16),
        scratch_shapes=[pltpu.VMEM((2, blk, LANES), F32), pltpu.VMEM((2, blk, 1), F32)],
        compiler_params=_params(("parallel", "parallel", "arbitrary")),
        name="sb_attention_prompt",
    )(bias, q, k, v, g)


def _sb_sample_kernel(pt_ref, bias_ref, q_ref, kn_ref, vn_ref, g_ref, *rest, nh, hd, pages):
    k_pages, v_pages = rest[:pages], rest[pages:2 * pages]
    o_ref, acc_ref, run_ref = rest[2 * pages:]
    del pt_ref
    ts, w = q_ref.shape[1:]
    page = k_pages[0].shape[1]
    rows = nh * ts
    j = pl.program_id(1)

    row_head = _idiv(_iota((rows, w), 0), ts)
    col_head = _idiv(_iota((rows, w), 1), hd)
    q = jnp.concatenate([q_ref[0].astype(F32)] * nh, axis=0)
    q_bd = jnp.where(row_head == col_head, q, 0.0).astype(BF16)
    rh = _idiv(_iota((rows, 1), 0), ts)
    bias = jnp.zeros((rows, 1), F32)
    for h in range(nh):
        bias = jnp.where(rh == h, bias_ref[h], bias)

    def tri(n):
        return (_iota((n, n), 0) >= _iota((n, n), 1)).astype(BF16)

    @pl.when(j == 0)
    def _():
        pad = jnp.zeros((LANES - ts, w), F32)
        kn = jnp.concatenate([kn_ref[0].astype(F32), pad], axis=0).astype(BF16)
        vn = jnp.concatenate([vn_ref[0].astype(F32), pad], axis=0).astype(BF16)
        z = _dot_nt(q_bd, kn) + bias
        tok = _iota((rows, LANES), 0) - _idiv(_iota((rows, LANES), 0), ts) * ts
        valid = _iota((rows, LANES), 1) < tok
        a, run = _sb_block(z, tri(LANES), jnp.zeros((rows, 1), F32), valid)
        run_ref[...] = run
        acc_ref[...] = _dot(a, vn)

    tri2 = tri(2 * page)
    for m in reversed(range(pages // 2)):
        kt = jnp.concatenate([k_pages[2 * m][...], k_pages[2 * m + 1][...]], axis=1).astype(BF16)
        vt = jnp.concatenate([v_pages[2 * m][...], v_pages[2 * m + 1][...]], axis=1).astype(BF16)
        z = _dot(q_bd, kt) + bias
        a, run = _sb_block(z, tri2, run_ref[...], None)
        run_ref[...] = run
        acc_ref[...] += _dot_nt(a, vt)

    @pl.when(j == pl.num_programs(1) - 1)
    def _():
        acc = jnp.where(row_head == col_head, acc_ref[...], 0.0)
        o = jnp.sum(acc.reshape(nh, ts, w), axis=0)
        same_head = (_idiv(_iota((w, w), 0), hd) == _idiv(_iota((w, w), 1), hd)).astype(BF16)
        ms = _dot_hp_exact_rhs(o * o, same_head) * (1.0 / hd)
        o_ref[0] = (o * lax.rsqrt(ms + EPS) * g_ref[...]).astype(BF16)


def _sb_sample(q, k_new, v_new, cache_k, cache_v, layer, page_table, bias, g, *, nh, hd):
    b, ts, w = q.shape
    n_pages = page_table.shape[1]
    page = cache_k.shape[3]
    pages = PAGES_PER_STEP if n_pages % PAGES_PER_STEP == 0 else 2
    assert n_pages % pages == 0 and pages % 2 == 0
    steps = n_pages // pages

    def tok_spec():
        return pl.BlockSpec((1, ts, w), lambda i, j, pt: (i, 0, 0))

    def page_spec(r):
        return pl.BlockSpec((None, None, w, page),
                            lambda i, j, pt: (layer, pt[i, (steps - 1 - j) * pages + r], 0, 0))

    grid_spec = pltpu.PrefetchScalarGridSpec(
        num_scalar_prefetch=1,
        grid=(b, steps),
        in_specs=[pl.BlockSpec(memory_space=pltpu.SMEM), tok_spec(), tok_spec(), tok_spec(),
                  pl.BlockSpec(g.shape, lambda i, j, pt: (0, 0))]
                 + [page_spec(r) for r in range(pages)] * 2,
        out_specs=tok_spec(),
        scratch_shapes=[pltpu.VMEM((nh * ts, w), F32), pltpu.VMEM((nh * ts, 1), F32)],
    )
    return pl.pallas_call(
        functools.partial(_sb_sample_kernel, nh=nh, hd=hd, pages=pages),
        grid_spec=grid_spec,
        out_shape=jax.ShapeDtypeStruct((b, ts, w), BF16),
        compiler_params=_params(("parallel", "arbitrary")),
        name="sb_attention_paged",
    )(page_table, bias, q, k_new, v_new, g, *([cache_k] * pages), *([cache_v] * pages))


def _outproj_kernel(x_ref, og_ref, os_ref, w_ref, gate_ref, shift_ref, scale_ref, g_ref, *rest):
    tb, ts, d = x_ref.shape
    gw = og_ref.shape[1]
    mixed = _dot(og_ref[...].astype(BF16), w_ref[0:gw, :]) + _dot(os_ref[...], w_ref[gw:, :])
    x1 = x_ref[...] + gate_ref[...] * mixed.reshape(tb, ts, d)
    h = _modulated_norm(x1, g_ref[...], scale_ref[...], shift_ref[...]).reshape(tb * ts, d)
    if len(rest) == 4:
        wr_ref, x1_ref, h_ref, lg_ref = rest
        lg_ref[...] = _dot_hp(h, wr_ref[...])
    else:
        x1_ref, h_ref = rest
    x1_ref[...] = x1
    h_ref[...] = h.astype(BF16)


def _output_projection(x, o_gla, o_sb, w_out, gate, shift, scale, g, w_router=None):
    b, t, d = x.shape
    n = b * t
    tb, ts = _row_tiling(b, t, ROW_TILE // 2)
    tm, nt = tb * ts, t // ts
    x_spec = pl.BlockSpec((tb, ts, d), lambda i, j: (i, j, 0))
    m_spec = pl.BlockSpec((tb, 1, d), lambda i, j: (i, 0, 0))

    def rows(width):
        return pl.BlockSpec((tm, width), lambda i, j: (i * nt + j, 0))

    def full(a):
        return pl.BlockSpec(a.shape, lambda i, j: (0,) * a.ndim)

    in_specs = [x_spec, rows(o_gla.shape[1]), rows(o_sb.shape[1]), full(w_out), m_spec, m_spec, m_spec, full(g)]
    args = [x, o_gla, o_sb, w_out, gate, shift, scale, g]
    out_specs = [x_spec, rows(d)]
    out_shape = [jax.ShapeDtypeStruct((b, t, d), F32), jax.ShapeDtypeStruct((n, d), BF16)]
    if w_router is not None:
        in_specs.append(full(w_router))
        args.append(w_router)
        out_specs.append(rows(LANES))
        out_shape.append(jax.ShapeDtypeStruct((n, LANES), F32))
    return pl.pallas_call(
        _outproj_kernel,
        grid=(b // tb, nt),
        in_specs=in_specs, out_specs=out_specs, out_shape=out_shape,
        compiler_params=_params(("parallel", "parallel")),
        name="output_projection_norm",
    )(*args)


def _ffn_kernel(h_ref, wg_ref, wv_ref, wd_ref, x_ref, gate_ref, o_ref, acc_ref):
    tb, ts, d = x_ref.shape
    f = pl.program_id(2)

    @pl.when(f == 0)
    def _():
        acc_ref[...] = jnp.zeros_like(acc_ref)

    h = h_ref[...]
    act = _silu(_dot(h, wg_ref[...])) * _dot(h, wv_ref[...])
    acc_ref[...] += _dot(act.astype(BF16), wd_ref[...])

    @pl.when(f == pl.num_programs(2) - 1)
    def _():
        o_ref[...] = x_ref[...] + gate_ref[...] * acc_ref[...].reshape(tb, ts, d)


def _ffn_tile(dff):
    for tf in (512, 256, 128):
        if dff % tf == 0:
            return tf
    raise ValueError(dff)


def _dense_ffn(x, h, w_up, w_down, gate):
    b, t, d = x.shape
    dff = w_down.shape[0]
    tf = _ffn_tile(dff)
    nf = dff // tf
    tb, ts = _row_tiling(b, t, ROW_TILE)
    tm, nt = tb * ts, t // ts
    x_spec = pl.BlockSpec((tb, ts, d), lambda i, j, f: (i, j, 0))
    return pl.pallas_call(
        _ffn_kernel,
        grid=(b // tb, nt, nf),
        in_specs=[pl.BlockSpec((tm, d), lambda i, j, f: (i * nt + j, 0)),
                  pl.BlockSpec((d, tf), lambda i, j, f: (0, f)),
                  pl.BlockSpec((d, tf), lambda i, j, f: (0, nf + f)),
                  pl.BlockSpec((tf, d), lambda i, j, f: (f, 0)),
                  x_spec,
                  pl.BlockSpec((tb, 1, d), lambda i, j, f: (i, 0, 0))],
        out_specs=x_spec,
        out_shape=jax.ShapeDtypeStruct((b, t, d), F32),
        scratch_shapes=[pltpu.VMEM((tm, d), F32)],
        compiler_params=_params(("parallel", "parallel", "arbitrary")),
        name="swiglu_ffn",
    )(h, w_up, w_up, w_down, x, gate)


def _top2_gates(logits, n_experts):
    lane = _iota(logits.shape, 1).astype(F32)
    neg = jnp.float32(-jnp.inf)
    l1 = jnp.where(lane < n_experts, logits, neg)
    m1 = jnp.max(l1, axis=-1, keepdims=True)
    i1 = jnp.min(jnp.where(l1 == m1, lane, float(LANES)), axis=-1, keepdims=True)
    l2 = jnp.where(lane == i1, neg, l1)
    m2 = jnp.max(l2, axis=-1, keepdims=True)
    i2 = jnp.min(jnp.where(l2 == m2, lane, float(LANES)), axis=-1, keepdims=True)
    p2 = jnp.exp(m2 - m1)
    w1 = 1.0 / (1.0 + p2)
    return jnp.where(lane == i1, w1, 0.0) + jnp.where(lane == i2, p2 * w1, 0.0)


def _moe_kernel(h_ref, lg_ref, wg_ref, wv_ref, wd_ref, x_ref, gate_ref, o_ref, acc_ref, gates_ref,
                *, n_experts):
    tb, ts, d = x_ref.shape
    e = pl.program_id(2)
    f = pl.program_id(3)

    @pl.when((e == 0) & (f == 0))
    def _():
        acc_ref[...] = jnp.zeros_like(acc_ref)
        gates_ref[...] = _top2_gates(lg_ref[...], n_experts)

    lane = _iota(gates_ref.shape, 1)
    ge = jnp.sum(jnp.where(lane == e, gates_ref[...], 0.0), axis=-1, keepdims=True)
    h = h_ref[...]
    act = _silu(_dot(h, wg_ref[0])) * _dot(h, wv_ref[0]) * ge
    acc_ref[...] += _dot(act.astype(BF16), wd_ref[0])

    @pl.when((e == n_experts - 1) & (f == pl.num_programs(3) - 1))
    def _():
        o_ref[...] = x_ref[...] + gate_ref[...] * acc_ref[...].reshape(tb, ts, d)


def _moe_ffn(x, h, logits, w_up, w_down, gate):
    b, t, d = x.shape
    n_experts, dff = w_down.shape[:2]
    tf = _ffn_tile(dff)
    nf = dff // tf
    tb, ts = _row_tiling(b, t, ROW_TILE)
    tm, nt = tb * ts, t // ts
    x_spec = pl.BlockSpec((tb, ts, d), lambda i, j, e, f: (i, j, 0))
    return pl.pallas_call(
        functools.partial(_moe_kernel, n_experts=n_experts),
        grid=(b // tb, nt, n_experts, nf),
        in_specs=[pl.BlockSpec((tm, d), lambda i, j, e, f: (i * nt + j, 0)),
                  pl.BlockSpec((tm, LANES), lambda i, j, e, f: (i * nt + j, 0)),
                  pl.BlockSpec((1, d, tf), lambda i, j, e, f: (e, 0, f)),
                  pl.BlockSpec((1, d, tf), lambda i, j, e, f: (e, 0, nf + f)),
                  pl.BlockSpec((1, tf, d), lambda i, j, e, f: (e, f, 0)),
                  x_spec,
                  pl.BlockSpec((tb, 1, d), lambda i, j, e, f: (i, 0, 0))],
        out_specs=x_spec,
        out_shape=jax.ShapeDtypeStruct((b, t, d), F32),
        scratch_shapes=[pltpu.VMEM((tm, d), F32), pltpu.VMEM((tm, LANES), F32)],
        compiler_params=_params(("parallel", "parallel", "arbitrary", "arbitrary")),
        name="moe_swiglu_ffn",
    )(h, logits, w_up, w_up, w_down, x, gate)


def _final_norm_kernel(x_ref, g_ref, o_ref):
    x = x_ref[...]
    ms = jnp.mean(x * x, axis=-1, keepdims=True)
    o_ref[...] = x * lax.rsqrt(ms + EPS) * g_ref[...]


def _final_norm(x, g):
    b, t, d = x.shape
    tb, ts = _row_tiling(b, t, ROW_TILE)
    x_spec = pl.BlockSpec((tb, ts, d), lambda i, j: (i, j, 0))
    return pl.pallas_call(
        _final_norm_kernel,
        grid=(b // tb, t // ts),
        in_specs=[x_spec, pl.BlockSpec(g.shape, lambda i, j: (0, 0, 0))],
        out_specs=x_spec,
        out_shape=jax.ShapeDtypeStruct((b, t, d), F32),
        compiler_params=_params(("parallel", "parallel")),
        name="final_rmsnorm",
    )(x, g)


def _arrange_w_in(w, widths):
    pieces, a = [], 0
    for wd in widths:
        pieces.append(w[:, a:a + wd])
        a += wd
    rank = pieces.pop(4)
    pieces.append(jnp.pad(rank, ((0, 0), (0, LANES - rank.shape[1]))))
    return jnp.concatenate(pieces, axis=1).astype(BF16)


def kernel(x_prompt, x_sample, c_prompt, c_sample, cache_sb_k, cache_sb_v, state_gla, page_table, w_ada, b_ada, norm_attn, norm_ffn, w_in, w_gate_up, b_gate, norm_gla, sb_bias, norm_sb, w_out, w_ff_up, w_ff_down, w_router, w_exp_up, w_exp_down, norm_final):
    depth, d = norm_attn.shape
    _, db, nh_gla, dk, dv = state_gla.shape
    _, n_phys, page, nh_sb, hd = cache_sb_k.shape
    rank, qk = w_gate_up.shape[1:]
    gw, sw = nh_gla * dv, nh_sb * hd
    widths = (qk, qk, gw, gw, rank, sw, sw, sw)
    bp, tp, _ = x_prompt.shape
    _, tsm, _ = x_sample.shape
    n_experts = w_router.shape[-1]

    mod = _modulation(jnp.concatenate([c_prompt, c_sample], axis=0), w_ada, b_ada)
    cache_k = cache_sb_k.transpose(0, 1, 3, 4, 2).reshape(depth, n_phys, sw, page)
    cache_v = cache_sb_v.transpose(0, 1, 3, 4, 2).reshape(depth, n_phys, sw, page)
    gla_zero = jnp.zeros((bp, nh_gla, dk, dv), F32)

    groups = {
        "prompt": dict(x=x_prompt, rows=slice(0, bp), s0=lambda l: gla_zero),
        "sample": dict(x=x_sample, rows=slice(bp, bp + db), s0=lambda l: state_gla[l]),
    }
    collected = {name: dict(k=[], v=[], s=[]) for name in groups}

    for l in range(depth):
        w_in_l = _arrange_w_in(w_in[l], widths)
        wg_l = jnp.pad(w_gate_up[l], ((0, LANES - rank), (0, 0)))
        bg_l = b_gate[l].reshape(1, qk)
        w_out_l = w_out[l].astype(BF16)
        moe = l % 2 == 1
        if moe:
            w_r = jnp.pad(w_router[l // 2], ((0, 0), (0, LANES - n_experts)))
            w_up_l = w_exp_up[l // 2].astype(BF16)
            w_down_l = w_exp_down[l // 2].astype(BF16)
        else:
            w_up_l = w_ff_up[l // 2].astype(BF16)
            w_down_l = w_ff_down[l // 2].astype(BF16)
        for name, grp in groups.items():
            x = grp["x"]
            b, t, _ = x.shape
            m = [mod[l, grp["rows"], i * d:(i + 1) * d].reshape(b, 1, d) for i in range(6)]
            shift1, scale1, gate1, shift2, scale2, gate2 = m
            gq, gk, gv, gg, la, sq, sk, sv, skb, svb = _input_projection(
                x, shift1, scale1, norm_attn[l].reshape(1, 1, d), w_in_l, wg_l, bg_l,
                qk=qk, gw=gw, sw=sw, dk=dk, hd=hd)
            o_gla, s_new = _gla(gq.reshape(b, t, qk), gk.reshape(b, t, qk), gv.reshape(b, t, gw),
                                la.reshape(b, t, qk), gg.reshape(b, t, gw), grp["s0"](l),
                                norm_gla[l].reshape(1, gw))
            g_sb = norm_sb[l].reshape(1, sw)
            if name == "prompt":
                o_sb = _sb_prompt(sq.reshape(b, t, sw), skb.reshape(b, t, sw), svb.reshape(b, t, sw),
                                  sb_bias[l], g_sb, hd=hd)
            else:
                o_sb = _sb_sample(sq.reshape(b, t, sw), skb.reshape(b, t, sw), svb.reshape(b, t, sw),
                                  cache_k, cache_v, l, page_table, sb_bias[l], g_sb, nh=nh_sb, hd=hd)
            res = _output_projection(x, o_gla.reshape(b * t, gw), o_sb.reshape(b * t, sw), w_out_l,
                                     gate1, shift2, scale2, norm_ffn[l].reshape(1, 1, d),
                                     w_r if moe else None)
            if moe:
                x1, h2, logits = res
                x = _moe_ffn(x1, h2, logits, w_up_l, w_down_l, gate2)
            else:
                x1, h2 = res
                x = _dense_ffn(x1, h2, w_up_l, w_down_l, gate2)
            grp["x"] = x
            collected[name]["k"].append(sk.reshape(b, t, nh_sb, hd))
            collected[name]["v"].append(sv.reshape(b, t, nh_sb, hd))
            collected[name]["s"].append(s_new)

    g_final = norm_final.reshape(1, 1, d)
    y_prompt = _final_norm(groups["prompt"]["x"], g_final)
    y_sample = _final_norm(groups["sample"]["x"], g_final)
    cp, cs = collected["prompt"], collected["sample"]
    return (y_prompt, y_sample, jnp.stack(cp["k"]), jnp.stack(cp["v"]), jnp.stack(cp["s"]),
            jnp.stack(cs["k"]), jnp.stack(cs["v"]), jnp.stack(cs["s"]))
```

```python
import functools

import jax
import jax.numpy as jnp
from jax import lax
from jax.experimental import pallas as pl
from jax.experimental.pallas import tpu as pltpu

F32 = jnp.float32
BF16 = jnp.bfloat16
EPS = 1e-6
GATE_TAU = 16.0
GLA_CHUNK = 64
GLA_SUB = 16
TOP_K = 2
LANES = 128
SB_BLOCK = 256
SB_QUERY_BLOCKS = 2
PAGES_PER_STEP = 16
LOG2E = 1.4426950408889634
ROW_TILE = 1024
VMEM_LIMIT = 56 * 1024 * 1024


def _params(sem):
    return pltpu.CompilerParams(dimension_semantics=sem, vmem_limit_bytes=VMEM_LIMIT)


def _dot(a, b):
    return jnp.dot(a, b, preferred_element_type=F32)


def _dot_nt(a, b):
    return lax.dot_general(a, b, (((1,), (1,)), ((), ())), preferred_element_type=F32)


def _dot_tn(a, b):
    return lax.dot_general(a, b, (((0,), (0,)), ((), ())), preferred_element_type=F32)


def _split(x):
    hi = x.astype(BF16)
    return hi, (x - hi.astype(F32)).astype(BF16)


def _dot_hp(a, b):
    ah, al = _split(a)
    bh, bl = _split(b)
    return _dot(ah, bh) + (_dot(ah, bl) + _dot(al, bh))


def _dot_hp_exact_rhs(a, b_bf16):
    ah, al = _split(a)
    return _dot(ah, b_bf16) + _dot(al, b_bf16)


def _silu(x):
    return x * (1.0 / (1.0 + jnp.exp(-x)))


def _log_sigmoid(x):
    return jnp.minimum(x, 0.0) - jnp.log1p(jnp.exp(-jnp.abs(x)))


def _idiv(x, n):
    if n & (n - 1) == 0:
        return lax.shift_right_logical(x, n.bit_length() - 1)
    return x // n


def _iota(shape, dim):
    return lax.broadcasted_iota(jnp.int32, shape, dim)


def _modulated_norm(x, g, scale, shift):
    ms = jnp.mean(x * x, axis=-1, keepdims=True)
    return (x * lax.rsqrt(ms + EPS) * g) * (1.0 + scale) + shift


def _row_tiling(b, t, target):
    if t >= target:
        assert t % target == 0
        return 1, target
    tb = min(b, max(1, target // t))
    assert b % tb == 0
    return tb, t


def _mod_kernel(c_ref, w_ref, b_ref, o_ref):
    o_ref[0] = _dot_hp(_silu(c_ref[...]), w_ref[0]) + b_ref[0]


def _modulation(c, w_ada, b_ada):
    depth, d, n = w_ada.shape
    bc = c.shape[0]
    tn = n // 4
    return pl.pallas_call(
        _mod_kernel,
        grid=(depth, n // tn),
        in_specs=[pl.BlockSpec((bc, d), lambda l, j: (0, 0)),
                  pl.BlockSpec((1, d, tn), lambda l, j: (l, 0, j)),
                  pl.BlockSpec((1, 1, tn), lambda l, j: (l, 0, j))],
        out_specs=pl.BlockSpec((1, bc, tn), lambda l, j: (l, 0, j)),
        out_shape=jax.ShapeDtypeStruct((depth, bc, n), F32),
        compiler_params=_params(("parallel", "parallel")),
        name="adaln_modulation",
    )(c, w_ada, b_ada.reshape(depth, 1, n))


def _inproj_kernel(x_ref, shift_ref, scale_ref, g_ref, w_ref, wg_ref, bg_ref,
                   gq_ref, gk_ref, gv_ref, gg_ref, la_ref, sq_ref, sk_ref, sv_ref, skb_ref, svb_ref,
                   *, qk, gw, sw, q_scale_gla, q_scale_sb):
    tb, ts, d = x_ref.shape
    h = _modulated_norm(x_ref[...], g_ref[...], scale_ref[...], shift_ref[...])
    hb = h.reshape(tb * ts, d).astype(BF16)
    off = [0]

    def proj(width):
        a = off[0]
        off[0] = a + width
        return _dot(hb, w_ref[:, a:a + width])

    gq_ref[...] = proj(qk) * q_scale_gla
    gk_ref[...] = proj(qk)
    gv_ref[...] = proj(gw)
    gg_ref[...] = proj(gw)
    sq_ref[...] = (proj(sw) * q_scale_sb).astype(BF16)
    sk = proj(sw)
    sk_ref[...] = sk
    skb_ref[...] = sk.astype(BF16)
    sv = proj(sw)
    sv_ref[...] = sv
    svb_ref[...] = sv.astype(BF16)
    ga = proj(LANES)
    gate = _dot_hp(ga, wg_ref[...]) + bg_ref[...]
    la_ref[...] = _log_sigmoid(gate) * (1.0 / GATE_TAU)


def _input_projection(x, shift, scale, g, w, wg, bg, *, qk, gw, sw, dk, hd):
    b, t, d = x.shape
    n = b * t
    tb, ts = _row_tiling(b, t, ROW_TILE // 2)
    tm = tb * ts
    nt = t // ts
    x_spec = pl.BlockSpec((tb, ts, d), lambda i, j: (i, j, 0))
    m_spec = pl.BlockSpec((tb, 1, d), lambda i, j: (i, 0, 0))

    def full(a):
        return pl.BlockSpec(a.shape, lambda i, j: (0,) * a.ndim)

    def out(width, dtype):
        return (pl.BlockSpec((tm, width), lambda i, j: (i * nt + j, 0)),
                jax.ShapeDtypeStruct((n, width), dtype))

    outs = [out(qk, F32), out(qk, F32), out(gw, F32), out(gw, F32), out(qk, F32),
            out(sw, BF16), out(sw, F32), out(sw, F32), out(sw, BF16), out(sw, BF16)]
    kern = functools.partial(_inproj_kernel, qk=qk, gw=gw, sw=sw,
                             q_scale_gla=dk ** -0.5, q_scale_sb=hd ** -0.5 * LOG2E)
    return pl.pallas_call(
        kern,
        grid=(b // tb, nt),
        in_specs=[x_spec, m_spec, m_spec, full(g), full(w), full(wg), full(bg)],
        out_specs=[o[0] for o in outs],
        out_shape=[o[1] for o in outs],
        compiler_params=_params(("parallel", "parallel")),
        name="norm_input_projection",
    )(x, shift, scale, g, w, wg, bg)


def _gla_kernel(q_ref, k_ref, v_ref, la_ref, gg_ref, s0_ref, g_ref, o_ref, s_out_ref, st_ref,
                *, chunk, sub):
    _, nh, dk, dv = s0_ref.shape
    tblock = q_ref.shape[1]
    qk, vw = nh * dk, nh * dv
    t = pl.program_id(1)

    @pl.when(t == 0)
    def _():
        rows = []
        for h in range(nh):
            rows.append(jnp.concatenate(
                [s0_ref[0, h] if hh == h else jnp.zeros((dk, dv), F32) for hh in range(nh)], axis=1))
        st_ref[...] = jnp.concatenate(rows, axis=0).T

    state_mask = _idiv(_iota((vw, qk), 0), dv) == _idiv(_iota((vw, qk), 1), dk)
    head_expand = (_idiv(_iota((qk, vw), 0), dk) == _idiv(_iota((qk, vw), 1), dv)).astype(BF16)
    key_mask = _idiv(_iota((nh * sub, qk), 0), sub) == _idiv(_iota((nh * sub, qk), 1), dk)
    val_mask = _idiv(_iota((nh * sub, vw), 0), sub) == _idiv(_iota((nh * sub, vw), 1), dv)
    tril = (_iota((chunk, chunk), 0) >= _iota((chunk, chunk), 1)).astype(BF16)
    row_id = _iota((chunk, 1), 0)
    pair_mask = _iota((sub, sub, 1), 0) >= _iota((sub, sub, 1), 1)
    n_sub = chunk // sub

    def one_chunk(ci, carry):
        r = pl.multiple_of(ci * chunk, chunk)
        q = q_ref[0, pl.ds(r, chunk), :]
        k = k_ref[0, pl.ds(r, chunk), :]
        v = v_ref[0, pl.ds(r, chunk), :]
        la_hi, la_lo = _split(la_ref[0, pl.ds(r, chunk), :])
        cum = _dot(tril, la_hi) + _dot(tril, la_lo)
        last = cum[chunk - 1:chunk, :]
        st = st_ref[...]

        o = _dot_nt((q * jnp.exp(cum)).astype(BF16), st.astype(BF16))
        k_dec = (k * jnp.exp(last - cum)).astype(BF16)
        upd = _dot_tn(v.astype(BF16), k_dec)
        st_ref[...] = st * jnp.exp(last) + jnp.where(state_mask, upd, 0.0)

        for j in range(n_sub - 1):
            a = j * sub
            ref_row = cum[a + sub - 1:a + sub, :]
            q_dec = (q * jnp.exp(jnp.minimum(cum - ref_row, 0.0))).astype(BF16)
            k_j = k[a:a + sub] * jnp.exp(ref_row - cum[a:a + sub])
            k_bd = jnp.where(key_mask, jnp.concatenate([k_j] * nh, axis=0), 0.0).astype(BF16)
            sc = _dot_nt(q_dec, k_bd)
            sc = jnp.where(row_id >= a + sub, sc, 0.0).astype(BF16)
            v_bd = jnp.where(val_mask, jnp.concatenate([v[a:a + sub]] * nh, axis=0), 0.0).astype(BF16)
            o = o + _dot(sc, v_bd)

        diag = []
        for j in range(n_sub):
            a = j * sub
            l_j, q_j, k_j, v_j = cum[a:a + sub], q[a:a + sub], k[a:a + sub], v[a:a + sub]
            dec = jnp.exp(jnp.minimum(l_j[:, None, :] - l_j[None, :, :], 0.0))
            p = jnp.where(pair_mask, (q_j[:, None, :] * k_j[None, :, :]) * dec, 0.0)
            sc = _dot(p.reshape(sub * sub, qk).astype(BF16), head_expand)
            diag.append(jnp.sum(sc.reshape(sub, sub, vw) * v_j[None, :, :], axis=1))
        o = o + (diag[0] if n_sub == 1 else jnp.concatenate(diag, axis=0))

        g = g_ref[...]
        gg = gg_ref[0, pl.ds(r, chunk), :]
        outs = []
        for h in range(nh):
            oh = o[:, h * dv:(h + 1) * dv]
            ms = jnp.mean(oh * oh, axis=-1, keepdims=True)
            outs.append(oh * lax.rsqrt(ms + EPS) * g[:, h * dv:(h + 1) * dv])
        o_ref[0, pl.ds(r, chunk), :] = jnp.concatenate(outs, axis=1) * _silu(gg)
        return carry

    lax.fori_loop(0, tblock // chunk, one_chunk, 0)

    @pl.when(t == pl.num_programs(1) - 1)
    def _():
        s_bd = st_ref[...].T
        for h in range(nh):
            s_out_ref[0, h] = s_bd[h * dk:(h + 1) * dk, h * dv:(h + 1) * dv]


def _gla(q, k, v, la, gg, s0, g):
    b, t, qk = q.shape
    vw = v.shape[-1]
    chunk = GLA_CHUNK if t % GLA_CHUNK == 0 else t
    sub = GLA_SUB if chunk % GLA_SUB == 0 else chunk
    tblock = min(t, 8 * chunk)
    assert t % tblock == 0

    def tok(width):
        return pl.BlockSpec((1, tblock, width), lambda i, j: (i, j, 0))

    s_spec = pl.BlockSpec((1,) + s0.shape[1:], lambda i, j: (i, 0, 0, 0))
    return pl.pallas_call(
        functools.partial(_gla_kernel, chunk=chunk, sub=sub),
        grid=(b, t // tblock),
        in_specs=[tok(qk), tok(qk), tok(vw), tok(qk), tok(vw), s_spec,
                  pl.BlockSpec(g.shape, lambda i, j: (0, 0))],
        out_specs=[tok(vw), s_spec],
        out_shape=[jax.ShapeDtypeStruct((b, t, vw), F32), jax.ShapeDtypeStruct(s0.shape, F32)],
        scratch_shapes=[pltpu.VMEM((vw, qk), F32)],
        compiler_params=_params(("parallel", "arbitrary")),
        name="gla_chunked",
    )(q, k, v, la, gg, s0, g)


def _softplus2(z2):
    neg_abs = lax.bitcast_convert_type(
        lax.bitcast_convert_type(z2, jnp.uint32) | jnp.uint32(0x80000000), F32)
    return jnp.maximum(z2, 0.0) + jnp.log2(1.0 + jnp.exp2(neg_abs))


def _sb_within(z2, tri, valid):
    sp = _softplus2(z2)
    if valid is not None:
        sp = jnp.where(valid, sp, 0.0)
    return _dot(sp.astype(BF16), tri)


def _sb_weights(z2, within, run, valid):
    a = jnp.exp2(z2 - (within + run))
    if valid is not None:
        a = jnp.where(valid, a, 0.0)
    return a.astype(BF16), run + within[:, 0:1]


def _sb_prompt_kernel(bias_ref, q_ref, k_ref, v_ref, g_ref, o_ref, acc_ref, run_ref, *, hd, kblk):
    qblk = q_ref.shape[0]
    n_sub = qblk // kblk
    p = pl.program_id(1)
    i = pl.program_id(2)
    q = q_ref[...]
    first = _iota((1, LANES), 1) < hd
    zero = jnp.zeros_like(q)
    q_heads = (jnp.where(first, q, zero), jnp.where(first, zero, q))
    biases = (bias_ref[2 * p] * LOG2E, bias_ref[2 * p + 1] * LOG2E)
    tri = (_iota((kblk, kblk), 0) >= _iota((kblk, kblk), 1)).astype(BF16)
    acc_ref[...] = jnp.zeros_like(acc_ref)
    run_ref[...] = jnp.zeros_like(run_ref)

    def sweep(blocks, valids):
        kv = []
        for jb in blocks:
            off = pl.multiple_of(jb * kblk, kblk)
            kv.append((k_ref[pl.ds(off, kblk), :], v_ref[pl.ds(off, kblk), :]))
        for h in range(2):
            zs = [_dot_nt(q_heads[h], kb) + biases[h] for kb, _ in kv]
            ws = [_sb_within(z, tri, valid) for z, valid in zip(zs, valids)]
            run = run_ref[h]
            out = None
            for z, w, valid, (_, vb) in zip(zs, ws, valids, kv):
                a, run = _sb_weights(z, w, run, valid)
                pv = _dot(a, vb)
                out = pv if out is None else out + pv
            run_ref[h] = run
            acc_ref[h] += out

    row = _iota((qblk, kblk), 0)
    col = _iota((qblk, kblk), 1)
    diag = list(reversed(range(n_sub)))
    sweep([i * n_sub + s for s in diag], [col + s * kblk < row for s in diag])

    def body(j, carry):
        base = (i - 1 - j) * n_sub
        sweep([base + s for s in diag], [None] * n_sub)
        return carry

    lax.fori_loop(0, i, body, 0)

    o = jnp.where(first, acc_ref[0], acc_ref[1])
    ss = o * o
    s0 = jnp.sum(jnp.where(first, ss, 0.0), axis=-1, keepdims=True)
    s1 = jnp.sum(jnp.where(first, 0.0, ss), axis=-1, keepdims=True)
    ms = jnp.where(first, s0, s1) * (1.0 / hd)
    o_ref[...] = (o * lax.rsqrt(ms + EPS) * g_ref[...]).astype(BF16)


def _sb_prompt(q, k, v, bias, g, *, hd):
    b, t, w = q.shape
    kblk = SB_BLOCK if t % SB_BLOCK == 0 else t
    blk = SB_QUERY_BLOCKS * kblk if t % (SB_QUERY_BLOCKS * kblk) == 0 else kblk
    n_pairs = w // LANES
    return pl.pallas_call(
        functools.partial(_sb_prompt_kernel, hd=hd, kblk=kblk),
        grid=(b, n_pairs, t // blk),
        in_specs=[pl.BlockSpec(memory_space=pltpu.SMEM),
                  pl.BlockSpec((None, blk, LANES), lambda bi, p, i: (bi, i, p)),
                  pl.BlockSpec((None, t, LANES), lambda bi, p, i: (bi, 0, p)),
                  pl.BlockSpec((None, t, LANES), lambda bi, p, i: (bi, 0, p)),
                  pl.BlockSpec((1, LANES), lambda bi, p, i: (0, p))],
        out_specs=pl.BlockSpec((None, blk, LANES), lambda bi, p, i: (bi, i, p)),
        out_shape=jax.ShapeDtypeStruct((b, t, w), BF16),
        scratch_shapes=[pltpu.VMEM((2, blk, LANES), F32), pltpu.VMEM((2, blk, 1), F32)],
        compiler_params=_params(("parallel", "parallel", "arbitrary")),
        name="sb_attention_prompt",
    )(bias, q, k, v, g)


def _sb_sample_kernel(pt_ref, bias_ref, q_ref, kn_ref, vn_ref, g_ref, *rest, nh, hd, pages):
    k_pages, v_pages = rest[:pages], rest[pages:2 * pages]
    o_ref, acc_ref, run_ref = rest[2 * pages:]
    del pt_ref
    ts, w = q_ref.shape[1:]
    page = k_pages[0].shape[1]
    rows = nh * ts
    j = pl.program_id(1)

    row_head = _idiv(_iota((rows, w), 0), ts)
    col_head = _idiv(_iota((rows, w), 1), hd)
    q = jnp.concatenate([q_ref[0].astype(F32)] * nh, axis=0)
    q_bd = jnp.where(row_head == col_head, q, 0.0).astype(BF16)
    rh = _idiv(_iota((rows, 1), 0), ts)
    bias = jnp.zeros((rows, 1), F32)
    for h in range(nh):
        bias = jnp.where(rh == h, bias_ref[h] * LOG2E, bias)

    def tri(n):
        return (_iota((n, n), 0) >= _iota((n, n), 1)).astype(BF16)

    @pl.when(j == 0)
    def _():
        pad = jnp.zeros((LANES - ts, w), F32)
        kn = jnp.concatenate([kn_ref[0].astype(F32), pad], axis=0).astype(BF16)
        vn = jnp.concatenate([vn_ref[0].astype(F32), pad], axis=0).astype(BF16)
        z = _dot_nt(q_bd, kn) + bias
        tok = _iota((rows, LANES), 0) - _idiv(_iota((rows, LANES), 0), ts) * ts
        valid = _iota((rows, LANES), 1) < tok
        a, run = _sb_weights(z, _sb_within(z, tri(LANES), valid), jnp.zeros((rows, 1), F32), valid)
        run_ref[...] = run
        acc_ref[...] = _dot(a, vn)

    blk = 2 * page
    tri2 = tri(blk)
    kt = jnp.concatenate([r[...] for r in k_pages], axis=1).astype(BF16)
    vt = jnp.concatenate([r[...] for r in v_pages], axis=1).astype(BF16)
    z = _dot(q_bd, kt) + bias
    n_blk = pages // 2
    zs = [z[:, m * blk:(m + 1) * blk] for m in range(n_blk)]
    ws = [_sb_within(zm, tri2, None) for zm in zs]
    run = run_ref[...]
    parts = [None] * n_blk
    for m in reversed(range(n_blk)):
        parts[m], run = _sb_weights(zs[m], ws[m], run, None)
    run_ref[...] = run
    acc_ref[...] += _dot_nt(jnp.concatenate(parts, axis=1), vt)

    @pl.when(j == pl.num_programs(1) - 1)
    def _():
        acc = jnp.where(row_head == col_head, acc_ref[...], 0.0)
        o = jnp.sum(acc.reshape(nh, ts, w), axis=0)
        same_head = (_idiv(_iota((w, w), 0), hd) == _idiv(_iota((w, w), 1), hd)).astype(BF16)
        ms = _dot_hp_exact_rhs(o * o, same_head) * (1.0 / hd)
        o_ref[0] = (o * lax.rsqrt(ms + EPS) * g_ref[...]).astype(BF16)


def _sb_sample(q, k_new, v_new, cache_k, cache_v, layer, page_table, bias, g, *, nh, hd):
    b, ts, w = q.shape
    n_pages = page_table.shape[1]
    page = cache_k.shape[3]
    pages = PAGES_PER_STEP if n_pages % PAGES_PER_STEP == 0 else 2
    assert n_pages % pages == 0 and pages % 2 == 0
    steps = n_pages // pages

    def tok_spec():
        return pl.BlockSpec((1, ts, w), lambda i, j, pt: (i, 0, 0))

    def page_spec(r):
        return pl.BlockSpec((None, None, w, page),
                            lambda i, j, pt: (layer, pt[i, (steps - 1 - j) * pages + r], 0, 0))

    grid_spec = pltpu.PrefetchScalarGridSpec(
        num_scalar_prefetch=1,
        grid=(b, steps),
        in_specs=[pl.BlockSpec(memory_space=pltpu.SMEM), tok_spec(), tok_spec(), tok_spec(),
                  pl.BlockSpec(g.shape, lambda i, j, pt: (0, 0))]
                 + [page_spec(r) for r in range(pages)] * 2,
        out_specs=tok_spec(),
        scratch_shapes=[pltpu.VMEM((nh * ts, w), F32), pltpu.VMEM((nh * ts, 1), F32)],
    )
    return pl.pallas_call(
        functools.partial(_sb_sample_kernel, nh=nh, hd=hd, pages=pages),
        grid_spec=grid_spec,
        out_shape=jax.ShapeDtypeStruct((b, ts, w), BF16),
        compiler_params=_params(("parallel", "arbitrary")),
        name="sb_attention_paged",
    )(page_table, bias, q, k_new, v_new, g, *([cache_k] * pages), *([cache_v] * pages))


def _outproj_kernel(x_ref, og_ref, os_ref, w_ref, gate_ref, shift_ref, scale_ref, g_ref, *rest):
    tb, ts, d = x_ref.shape
    gw = og_ref.shape[1]
    mixed = _dot(og_ref[...].astype(BF16), w_ref[0:gw, :]) + _dot(os_ref[...], w_ref[gw:, :])
    x1 = x_ref[...] + gate_ref[...] * mixed.reshape(tb, ts, d)
    h = _modulated_norm(x1, g_ref[...], scale_ref[...], shift_ref[...]).reshape(tb * ts, d)
    if len(rest) == 4:
        wr_ref, x1_ref, h_ref, lg_ref = rest
        lg_ref[...] = _dot_hp(h, wr_ref[...])
    else:
        x1_ref, h_ref = rest
    x1_ref[...] = x1
    h_ref[...] = h.astype(BF16)


def _output_projection(x, o_gla, o_sb, w_out, gate, shift, scale, g, w_router=None):
    b, t, d = x.shape
    n = b * t
    tb, ts = _row_tiling(b, t, ROW_TILE // 2)
    tm, nt = tb * ts, t // ts
    x_spec = pl.BlockSpec((tb, ts, d), lambda i, j: (i, j, 0))
    m_spec = pl.BlockSpec((tb, 1, d), lambda i, j: (i, 0, 0))

    def rows(width):
        return pl.BlockSpec((tm, width), lambda i, j: (i * nt + j, 0))

    def full(a):
        return pl.BlockSpec(a.shape, lambda i, j: (0,) * a.ndim)

    in_specs = [x_spec, rows(o_gla.shape[1]), rows(o_sb.shape[1]), full(w_out), m_spec, m_spec, m_spec, full(g)]
    args = [x, o_gla, o_sb, w_out, gate, shift, scale, g]
    out_specs = [x_spec, rows(d)]
    out_shape = [jax.ShapeDtypeStruct((b, t, d), F32), jax.ShapeDtypeStruct((n, d), BF16)]
    if w_router is not None:
        in_specs.append(full(w_router))
        args.append(w_router)
        out_specs.append(rows(LANES))
        out_shape.append(jax.ShapeDtypeStruct((n, LANES), F32))
    return pl.pallas_call(
        _outproj_kernel,
        grid=(b // tb, nt),
        in_specs=in_specs, out_specs=out_specs, out_shape=out_shape,
        compiler_params=_params(("parallel", "parallel")),
        name="output_projection_norm",
    )(*args)


def _ffn_kernel(h_ref, wg_ref, wv_ref, wd_ref, x_ref, gate_ref, o_ref, acc_ref):
    tb, ts, d = x_ref.shape
    f = pl.program_id(2)

    @pl.when(f == 0)
    def _():
        acc_ref[...] = jnp.zeros_like(acc_ref)

    h = h_ref[...]
    act = _silu(_dot(h, wg_ref[...])) * _dot(h, wv_ref[...])
    acc_ref[...] += _dot(act.astype(BF16), wd_ref[...])

    @pl.when(f == pl.num_programs(2) - 1)
    def _():
        o_ref[...] = x_ref[...] + gate_ref[...] * acc_ref[...].reshape(tb, ts, d)


def _ffn_tile(dff):
    for tf in (512, 256, 128):
        if dff % tf == 0:
            return tf
    raise ValueError(dff)


def _dense_ffn(x, h, w_up, w_down, gate):
    b, t, d = x.shape
    dff = w_down.shape[0]
    tf = _ffn_tile(dff)
    nf = dff // tf
    tb, ts = _row_tiling(b, t, ROW_TILE)
    tm, nt = tb * ts, t // ts
    x_spec = pl.BlockSpec((tb, ts, d), lambda i, j, f: (i, j, 0))
    return pl.pallas_call(
        _ffn_kernel,
        grid=(b // tb, nt, nf),
        in_specs=[pl.BlockSpec((tm, d), lambda i, j, f: (i * nt + j, 0)),
                  pl.BlockSpec((d, tf), lambda i, j, f: (0, f)),
                  pl.BlockSpec((d, tf), lambda i, j, f: (0, nf + f)),
                  pl.BlockSpec((tf, d), lambda i, j, f: (f, 0)),
                  x_spec,
                  pl.BlockSpec((tb, 1, d), lambda i, j, f: (i, 0, 0))],
        out_specs=x_spec,
        out_shape=jax.ShapeDtypeStruct((b, t, d), F32),
        scratch_shapes=[pltpu.VMEM((tm, d), F32)],
        compiler_params=_params(("parallel", "parallel", "arbitrary")),
        name="swiglu_ffn",
    )(h, w_up, w_up, w_down, x, gate)


def _top2_gates(logits, n_experts):
    lane = _iota(logits.shape, 1).astype(F32)
    neg = jnp.float32(-jnp.inf)
    l1 = jnp.where(lane < n_experts, logits, neg)
    m1 = jnp.max(l1, axis=-1, keepdims=True)
    i1 = jnp.min(jnp.where(l1 == m1, lane, float(LANES)), axis=-1, keepdims=True)
    l2 = jnp.where(lane == i1, neg, l1)
    m2 = jnp.max(l2, axis=-1, keepdims=True)
    i2 = jnp.min(jnp.where(l2 == m2, lane, float(LANES)), axis=-1, keepdims=True)
    p2 = jnp.exp(m2 - m1)
    w1 = 1.0 / (1.0 + p2)
    return jnp.where(lane == i1, w1, 0.0) + jnp.where(lane == i2, p2 * w1, 0.0)


def _moe_kernel(h_ref, lg_ref, wg_ref, wv_ref, wd_ref, x_ref, gate_ref, o_ref, acc_ref, gates_ref,
                *, n_experts):
    tb, ts, d = x_ref.shape
    e = pl.program_id(2)
    f = pl.program_id(3)

    @pl.when((e == 0) & (f == 0))
    def _():
        acc_ref[...] = jnp.zeros_like(acc_ref)
        gates_ref[...] = _top2_gates(lg_ref[...], n_experts)

    lane = _iota(gates_ref.shape, 1)
    ge = jnp.sum(jnp.where(lane == e, gates_ref[...], 0.0), axis=-1, keepdims=True)
    h = h_ref[...]
    act = _silu(_dot(h, wg_ref[0])) * _dot(h, wv_ref[0]) * ge
    acc_ref[...] += _dot(act.astype(BF16), wd_ref[0])

    @pl.when((e == n_experts - 1) & (f == pl.num_programs(3) - 1))
    def _():
        o_ref[...] = x_ref[...] + gate_ref[...] * acc_ref[...].reshape(tb, ts, d)


def _moe_ffn(x, h, logits, w_up, w_down, gate):
    b, t, d = x.shape
    n_experts, dff = w_down.shape[:2]
    tf = _ffn_tile(dff)
    nf = dff // tf
    tb, ts = _row_tiling(b, t, ROW_TILE)
    tm, nt = tb * ts, t // ts
    x_spec = pl.BlockSpec((tb, ts, d), lambda i, j, e, f: (i, j, 0))
    return pl.pallas_call(
        functools.partial(_moe_kernel, n_experts=n_experts),
        grid=(b // tb, nt, n_experts, nf),
        in_specs=[pl.BlockSpec((tm, d), lambda i, j, e, f: (i * nt + j, 0)),
                  pl.BlockSpec((tm, LANES), lambda i, j, e, f: (i * nt + j, 0)),
                  pl.BlockSpec((1, d, tf), lambda i, j, e, f: (e, 0, f)),
                  pl.BlockSpec((1, d, tf), lambda i, j, e, f: (e, 0, nf + f)),
                  pl.BlockSpec((1, tf, d), lambda i, j, e, f: (e, f, 0)),
                  x_spec,
                  pl.BlockSpec((tb, 1, d), lambda i, j, e, f: (i, 0, 0))],
        out_specs=x_spec,
        out_shape=jax.ShapeDtypeStruct((b, t, d), F32),
        scratch_shapes=[pltpu.VMEM((tm, d), F32), pltpu.VMEM((tm, LANES), F32)],
        compiler_params=_params(("parallel", "parallel", "arbitrary", "arbitrary")),
        name="moe_swiglu_ffn",
    )(h, logits, w_up, w_up, w_down, x, gate)


def _final_norm_kernel(x_ref, g_ref, o_ref):
    x = x_ref[...]
    ms = jnp.mean(x * x, axis=-1, keepdims=True)
    o_ref[...] = x * lax.rsqrt(ms + EPS) * g_ref[...]


def _final_norm(x, g):
    b, t, d = x.shape
    tb, ts = _row_tiling(b, t, ROW_TILE)
    x_spec = pl.BlockSpec((tb, ts, d), lambda i, j: (i, j, 0))
    return pl.pallas_call(
        _final_norm_kernel,
        grid=(b // tb, t // ts),
        in_specs=[x_spec, pl.BlockSpec(g.shape, lambda i, j: (0, 0, 0))],
        out_specs=x_spec,
        out_shape=jax.ShapeDtypeStruct((b, t, d), F32),
        compiler_params=_params(("parallel", "parallel")),
        name="final_rmsnorm",
    )(x, g)


def _arrange_w_in(w, widths):
    pieces, a = [], 0
    for wd in widths:
        pieces.append(w[:, a:a + wd])
        a += wd
    rank = pieces.pop(4)
    pieces.append(jnp.pad(rank, ((0, 0), (0, LANES - rank.shape[1]))))
    return jnp.concatenate(pieces, axis=1).astype(BF16)


def kernel(x_prompt, x_sample, c_prompt, c_sample, cache_sb_k, cache_sb_v, state_gla, page_table, w_ada, b_ada, norm_attn, norm_ffn, w_in, w_gate_up, b_gate, norm_gla, sb_bias, norm_sb, w_out, w_ff_up, w_ff_down, w_router, w_exp_up, w_exp_down, norm_final):
    depth, d = norm_attn.shape
    _, db, nh_gla, dk, dv = state_gla.shape
    _, n_phys, page, nh_sb, hd = cache_sb_k.shape
    rank, qk = w_gate_up.shape[1:]
    gw, sw = nh_gla * dv, nh_sb * hd
    widths = (qk, qk, gw, gw, rank, sw, sw, sw)
    bp, tp, _ = x_prompt.shape
    _, tsm, _ = x_sample.shape
    n_experts = w_router.shape[-1]

    mod = _modulation(jnp.concatenate([c_prompt, c_sample], axis=0), w_ada, b_ada)
    cache_k = cache_sb_k.transpose(0, 1, 3, 4, 2).reshape(depth, n_phys, sw, page)
    cache_v = cache_sb_v.transpose(0, 1, 3, 4, 2).reshape(depth, n_phys, sw, page)
    gla_zero = jnp.zeros((bp, nh_gla, dk, dv), F32)

    groups = {
        "prompt": dict(x=x_prompt, rows=slice(0, bp), s0=lambda l: gla_zero),
        "sample": dict(x=x_sample, rows=slice(bp, bp + db), s0=lambda l: state_gla[l]),
    }
    collected = {name: dict(k=[], v=[], s=[]) for name in groups}

    for l in range(depth):
        w_in_l = _arrange_w_in(w_in[l], widths)
        wg_l = jnp.pad(w_gate_up[l], ((0, LANES - rank), (0, 0)))
        bg_l = b_gate[l].reshape(1, qk)
        w_out_l = w_out[l].astype(BF16)
        moe = l % 2 == 1
        if moe:
            w_r = jnp.pad(w_router[l // 2], ((0, 0), (0, LANES - n_experts)))
            w_up_l = w_exp_up[l // 2].astype(BF16)
            w_down_l = w_exp_down[l // 2].astype(BF16)
        else:
            w_up_l = w_ff_up[l // 2].astype(BF16)
            w_down_l = w_ff_down[l // 2].astype(BF16)
        for name, grp in groups.items():
            x = grp["x"]
            b, t, _ = x.shape
            m = [mod[l, grp["rows"], i * d:(i + 1) * d].reshape(b, 1, d) for i in range(6)]
            shift1, scale1, gate1, shift2, scale2, gate2 = m
            gq, gk, gv, gg, la, sq, sk, sv, skb, svb = _input_projection(
                x, shift1, scale1, norm_attn[l].reshape(1, 1, d), w_in_l, wg_l, bg_l,
                qk=qk, gw=gw, sw=sw, dk=dk, hd=hd)
            o_gla, s_new = _gla(gq.reshape(b, t, qk), gk.reshape(b, t, qk), gv.reshape(b, t, gw),
                                la.reshape(b, t, qk), gg.reshape(b, t, gw), grp["s0"](l),
                                norm_gla[l].reshape(1, gw))
            g_sb = norm_sb[l].reshape(1, sw)
            if name == "prompt":
                o_sb = _sb_prompt(sq.reshape(b, t, sw), skb.reshape(b, t, sw), svb.reshape(b, t, sw),
                                  sb_bias[l], g_sb, hd=hd)
            else:
                o_sb = _sb_sample(sq.reshape(b, t, sw), skb.reshape(b, t, sw), svb.reshape(b, t, sw),
                                  cache_k, cache_v, l, page_table, sb_bias[l], g_sb, nh=nh_sb, hd=hd)
            res = _output_projection(x, o_gla.reshape(b * t, gw), o_sb.reshape(b * t, sw), w_out_l,
                                     gate1, shift2, scale2, norm_ffn[l].reshape(1, 1, d),
                                     w_r if moe else None)
            if moe:
                x1, h2, logits = res
                x = _moe_ffn(x1, h2, logits, w_up_l, w_down_l, gate2)
            else:
                x1, h2 = res
                x = _dense_ffn(x1, h2, w_up_l, w_down_l, gate2)
            grp["x"] = x
            collected[name]["k"].append(sk.reshape(b, t, nh_sb, hd))
            collected[name]["v"].append(sv.reshape(b, t, nh_sb, hd))
            collected[name]["s"].append(s_new)

    g_final = norm_final.reshape(1, 1, d)
    y_prompt = _final_norm(groups["prompt"]["x"], g_final)
    y_sample = _final_norm(groups["sample"]["x"], g_final)
    cp, cs = collected["prompt"], collected["sample"]
    return (y_prompt, y_sample, jnp.stack(cp["k"]), jnp.stack(cp["v"]), jnp.stack(cp["s"]),
            jnp.stack(cs["k"]), jnp.stack(cs["v"]), jnp.stack(cs["s"]))
```

```python
import functools

import jax
import jax.numpy as jnp
from jax import lax
from jax.experimental import pallas as pl
from jax.experimental.pallas import tpu as pltpu

F32 = jnp.float32
BF16 = jnp.bfloat16
EPS = 1e-6
GATE_TAU = 16.0
GLA_CHUNK = 64
GLA_SUB = 16
TOP_K = 2
LANES = 128
SB_BLOCK = 256
SB_QUERY_BLOCKS = 2
SB_SWEEP_BLOCKS = 4
PAGES_PER_STEP = 16
LOG2E = 1.4426950408889634
ROW_TILE = 1024
MOE_CHUNK_ROWS = 256
VMEM_LIMIT = 56 * 1024 * 1024


def _params(sem):
    return pltpu.CompilerParams(dimension_semantics=sem, vmem_limit_bytes=VMEM_LIMIT)


def _dot(a, b):
    return jnp.dot(a, b, preferred_element_type=F32)


def _dot_nt(a, b):
    return lax.dot_general(a, b, (((1,), (1,)), ((), ())), preferred_element_type=F32)


def _dot_tn(a, b):
    return lax.dot_general(a, b, (((0,), (0,)), ((), ())), preferred_element_type=F32)


def _split(x):
    hi = x.astype(BF16)
    return hi, (x - hi.astype(F32)).astype(BF16)


def _dot_hp(a, b):
    ah, al = _split(a)
    bh, bl = _split(b)
    return _dot(ah, bh) + (_dot(ah, bl) + _dot(al, bh))


def _dot_hp_exact_rhs(a, b_bf16):
    ah, al = _split(a)
    return _dot(ah, b_bf16) + _dot(al, b_bf16)


def _silu(x):
    return x * (1.0 / (1.0 + jnp.exp(-x)))


def _log_sigmoid(x):
    return jnp.minimum(x, 0.0) - jnp.log1p(jnp.exp(-jnp.abs(x)))


def _idiv(x, n):
    if n & (n - 1) == 0:
        return lax.shift_right_logical(x, n.bit_length() - 1)
    return x // n


def _iota(shape, dim):
    return lax.broadcasted_iota(jnp.int32, shape, dim)


def _modulated_norm(x, g, scale, shift):
    ms = jnp.mean(x * x, axis=-1, keepdims=True)
    return (x * lax.rsqrt(ms + EPS) * g) * (1.0 + scale) + shift


def _row_tiling(b, t, target):
    if t >= target:
        assert t % target == 0
        return 1, target
    tb = min(b, max(1, target // t))
    assert b % tb == 0
    return tb, t


def _mod_kernel(c_ref, w_ref, b_ref, o_ref):
    o_ref[0] = _dot_hp(_silu(c_ref[...]), w_ref[0]) + b_ref[0]


def _modulation(c, w_ada, b_ada):
    depth, d, n = w_ada.shape
    bc = c.shape[0]
    tn = n // 4
    return pl.pallas_call(
        _mod_kernel,
        grid=(depth, n // tn),
        in_specs=[pl.BlockSpec((bc, d), lambda l, j: (0, 0)),
                  pl.BlockSpec((1, d, tn), lambda l, j: (l, 0, j)),
                  pl.BlockSpec((1, 1, tn), lambda l, j: (l, 0, j))],
        out_specs=pl.BlockSpec((1, bc, tn), lambda l, j: (l, 0, j)),
        out_shape=jax.ShapeDtypeStruct((depth, bc, n), F32),
        compiler_params=_params(("parallel", "parallel")),
        name="adaln_modulation",
    )(c, w_ada, b_ada.reshape(depth, 1, n))


def _inproj_kernel(x_ref, shift_ref, scale_ref, g_ref, w_ref, wg_ref, bg_ref,
                   gq_ref, gk_ref, gv_ref, gg_ref, la_ref, sq_ref, sk_ref, sv_ref, skb_ref, svb_ref,
                   *, qk, gw, sw, q_scale_gla, q_scale_sb):
    tb, ts, d = x_ref.shape
    h = _modulated_norm(x_ref[...], g_ref[...], scale_ref[...], shift_ref[...])
    hb = h.reshape(tb * ts, d).astype(BF16)
    off = [0]

    def proj(width):
        a = off[0]
        off[0] = a + width
        return _dot(hb, w_ref[:, a:a + width])

    gq_ref[...] = proj(qk) * q_scale_gla
    gk_ref[...] = proj(qk)
    gv_ref[...] = proj(gw)
    gg_ref[...] = proj(gw)
    sq_ref[...] = (proj(sw) * q_scale_sb).astype(BF16)
    sk = proj(sw)
    sk_ref[...] = sk
    skb_ref[...] = sk.astype(BF16)
    sv = proj(sw)
    sv_ref[...] = sv
    svb_ref[...] = sv.astype(BF16)
    ga = proj(LANES)
    gate = _dot_hp(ga, wg_ref[...]) + bg_ref[...]
    la_ref[...] = _log_sigmoid(gate) * (1.0 / GATE_TAU)


def _input_projection(x, shift, scale, g, w, wg, bg, *, qk, gw, sw, dk, hd):
    b, t, d = x.shape
    n = b * t
    tb, ts = _row_tiling(b, t, ROW_TILE // 2)
    tm = tb * ts
    nt = t // ts
    x_spec = pl.BlockSpec((tb, ts, d), lambda i, j: (i, j, 0))
    m_spec = pl.BlockSpec((tb, 1, d), lambda i, j: (i, 0, 0))

    def full(a):
        return pl.BlockSpec(a.shape, lambda i, j: (0,) * a.ndim)

    def out(width, dtype):
        return (pl.BlockSpec((tm, width), lambda i, j: (i * nt + j, 0)),
                jax.ShapeDtypeStruct((n, width), dtype))

    outs = [out(qk, F32), out(qk, F32), out(gw, F32), out(gw, F32), out(qk, F32),
            out(sw, BF16), out(sw, F32), out(sw, F32), out(sw, BF16), out(sw, BF16)]
    kern = functools.partial(_inproj_kernel, qk=qk, gw=gw, sw=sw,
                             q_scale_gla=dk ** -0.5, q_scale_sb=hd ** -0.5 * LOG2E)
    return pl.pallas_call(
        kern,
        grid=(b // tb, nt),
        in_specs=[x_spec, m_spec, m_spec, full(g), full(w), full(wg), full(bg)],
        out_specs=[o[0] for o in outs],
        out_shape=[o[1] for o in outs],
        compiler_params=_params(("parallel", "parallel")),
        name="norm_input_projection",
    )(x, shift, scale, g, w, wg, bg)


def _gla_kernel(q_ref, k_ref, v_ref, la_ref, gg_ref, s0_ref, g_ref, o_ref, s_out_ref, st_ref,
                *, chunk, sub):
    _, nh, dk, dv = s0_ref.shape
    tblock = q_ref.shape[1]
    qk, vw = nh * dk, nh * dv
    t = pl.program_id(1)

    @pl.when(t == 0)
    def _():
        rows = []
        for h in range(nh):
            rows.append(jnp.concatenate(
                [s0_ref[0, h] if hh == h else jnp.zeros((dk, dv), F32) for hh in range(nh)], axis=1))
        st_ref[...] = jnp.concatenate(rows, axis=0).T

    state_mask = _idiv(_iota((vw, qk), 0), dv) == _idiv(_iota((vw, qk), 1), dk)
    head_expand = (_idiv(_iota((qk, vw), 0), dk) == _idiv(_iota((qk, vw), 1), dv)).astype(BF16)
    key_mask = _idiv(_iota((nh * sub, qk), 0), sub) == _idiv(_iota((nh * sub, qk), 1), dk)
    val_mask = _idiv(_iota((nh * sub, vw), 0), sub) == _idiv(_iota((nh * sub, vw), 1), dv)
    tril = (_iota((chunk, chunk), 0) >= _iota((chunk, chunk), 1)).astype(BF16)
    row_id = _iota((chunk, 1), 0)
    pair_mask = _iota((sub, sub, 1), 0) >= _iota((sub, sub, 1), 1)
    n_sub = chunk // sub

    def one_chunk(ci, carry):
        r = pl.multiple_of(ci * chunk, chunk)
        q = q_ref[0, pl.ds(r, chunk), :]
        k = k_ref[0, pl.ds(r, chunk), :]
        v = v_ref[0, pl.ds(r, chunk), :]
        la_hi, la_lo = _split(la_ref[0, pl.ds(r, chunk), :])
        cum = _dot(tril, la_hi) + _dot(tril, la_lo)
        last = cum[chunk - 1:chunk, :]
        st = st_ref[...]

        o = _dot_nt((q * jnp.exp(cum)).astype(BF16), st.astype(BF16))
        k_dec = (k * jnp.exp(last - cum)).astype(BF16)
        upd = _dot_tn(v.astype(BF16), k_dec)
        st_ref[...] = st * jnp.exp(last) + jnp.where(state_mask, upd, 0.0)

        for j in range(n_sub - 1):
            a = j * sub
            ref_row = cum[a + sub - 1:a + sub, :]
            q_dec = (q * jnp.exp(jnp.minimum(cum - ref_row, 0.0))).astype(BF16)
            k_j = k[a:a + sub] * jnp.exp(ref_row - cum[a:a + sub])
            k_bd = jnp.where(key_mask, jnp.concatenate([k_j] * nh, axis=0), 0.0).astype(BF16)
            sc = _dot_nt(q_dec, k_bd)
            sc = jnp.where(row_id >= a + sub, sc, 0.0).astype(BF16)
            v_bd = jnp.where(val_mask, jnp.concatenate([v[a:a + sub]] * nh, axis=0), 0.0).astype(BF16)
            o = o + _dot(sc, v_bd)

        diag = []
        for j in range(n_sub):
            a = j * sub
            l_j, q_j, k_j, v_j = cum[a:a + sub], q[a:a + sub], k[a:a + sub], v[a:a + sub]
            dec = jnp.exp(jnp.minimum(l_j[:, None, :] - l_j[None, :, :], 0.0))
            p = jnp.where(pair_mask, (q_j[:, None, :] * k_j[None, :, :]) * dec, 0.0)
            sc = _dot(p.reshape(sub * sub, qk).astype(BF16), head_expand)
            diag.append(jnp.sum(sc.reshape(sub, sub, vw) * v_j[None, :, :], axis=1))
        o = o + (diag[0] if n_sub == 1 else jnp.concatenate(diag, axis=0))

        g = g_ref[...]
        gg = gg_ref[0, pl.ds(r, chunk), :]
        outs = []
        for h in range(nh):
            oh = o[:, h * dv:(h + 1) * dv]
            ms = jnp.mean(oh * oh, axis=-1, keepdims=True)
            outs.append(oh * lax.rsqrt(ms + EPS) * g[:, h * dv:(h + 1) * dv])
        o_ref[0, pl.ds(r, chunk), :] = jnp.concatenate(outs, axis=1) * _silu(gg)
        return carry

    lax.fori_loop(0, tblock // chunk, one_chunk, 0)

    @pl.when(t == pl.num_programs(1) - 1)
    def _():
        s_bd = st_ref[...].T
        for h in range(nh):
            s_out_ref[0, h] = s_bd[h * dk:(h + 1) * dk, h * dv:(h + 1) * dv]


def _gla(q, k, v, la, gg, s0, g):
    b, t, qk = q.shape
    vw = v.shape[-1]
    chunk = GLA_CHUNK if t % GLA_CHUNK == 0 else t
    sub = GLA_SUB if chunk % GLA_SUB == 0 else chunk
    tblock = min(t, 8 * chunk)
    assert t % tblock == 0

    def tok(width):
        return pl.BlockSpec((1, tblock, width), lambda i, j: (i, j, 0))

    s_spec = pl.BlockSpec((1,) + s0.shape[1:], lambda i, j: (i, 0, 0, 0))
    return pl.pallas_call(
        functools.partial(_gla_kernel, chunk=chunk, sub=sub),
        grid=(b, t // tblock),
        in_specs=[tok(qk), tok(qk), tok(vw), tok(qk), tok(vw), s_spec,
                  pl.BlockSpec(g.shape, lambda i, j: (0, 0))],
        out_specs=[tok(vw), s_spec],
        out_shape=[jax.ShapeDtypeStruct((b, t, vw), F32), jax.ShapeDtypeStruct(s0.shape, F32)],
        scratch_shapes=[pltpu.VMEM((vw, qk), F32)],
        compiler_params=_params(("parallel", "arbitrary")),
        name="gla_chunked",
    )(q, k, v, la, gg, s0, g)


def _softplus2(z2):
    neg_abs = lax.bitcast_convert_type(
        lax.bitcast_convert_type(z2, jnp.uint32) | jnp.uint32(0x80000000), F32)
    return jnp.maximum(z2, 0.0) + jnp.log2(1.0 + jnp.exp2(neg_abs))


def _sb_within(z2, tri, valid):
    sp = _softplus2(z2)
    if valid is not None:
        sp = jnp.where(valid, sp, 0.0)
    return _dot(sp.astype(BF16), tri)


def _sb_weights(z2, within, run, valid):
    a = jnp.exp2(z2 - (within + run))
    if valid is not None:
        a = jnp.where(valid, a, 0.0)
    return a.astype(BF16), run + within[:, 0:1]


def _sb_prompt_kernel(bias_ref, q_ref, k_ref, v_ref, g_ref, o_ref, acc_ref, run_ref, *, hd, kblk):
    qblk = q_ref.shape[0]
    n_sub = qblk // kblk
    p = pl.program_id(1)
    i = pl.program_id(2)
    q = q_ref[...]
    first = _iota((1, LANES), 1) < hd
    zero = jnp.zeros_like(q)
    q_heads = (jnp.where(first, q, zero), jnp.where(first, zero, q))
    biases = (bias_ref[2 * p] * LOG2E, bias_ref[2 * p + 1] * LOG2E)
    tri = (_iota((kblk, kblk), 0) >= _iota((kblk, kblk), 1)).astype(BF16)
    acc_ref[...] = jnp.zeros_like(acc_ref)
    run_ref[...] = jnp.zeros_like(run_ref)

    def sweep(blocks, valids):
        kv = []
        for jb in blocks:
            off = pl.multiple_of(jb * kblk, kblk)
            kv.append((k_ref[pl.ds(off, kblk), :], v_ref[pl.ds(off, kblk), :]))
        for h in range(2):
            zs = [_dot_nt(q_heads[h], kb) + biases[h] for kb, _ in kv]
            ws = [_sb_within(z, tri, valid) for z, valid in zip(zs, valids)]
            run = run_ref[h]
            out = None
            for z, w, valid, (_, vb) in zip(zs, ws, valids, kv):
                a, run = _sb_weights(z, w, run, valid)
                pv = _dot(a, vb)
                out = pv if out is None else out + pv
            run_ref[h] = run
            acc_ref[h] += out

    row = _iota((qblk, kblk), 0)
    col = _iota((qblk, kblk), 1)
    diag = list(reversed(range(n_sub)))
    sweep([i * n_sub + s for s in diag], [col + s * kblk < row for s in diag])

    groups = SB_SWEEP_BLOCKS // n_sub if SB_SWEEP_BLOCKS % n_sub == 0 else 1
    per_iter = groups * n_sub
    odd = lax.rem(i, groups)
    for r in range(1, groups):
        @pl.when(odd >= r)
        def _():
            base = (i - r) * n_sub
            sweep([base + s for s in diag], [None] * n_sub)
    top = (i - odd) * n_sub
    desc = list(reversed(range(per_iter)))

    def body(j, carry):
        base = top - (j + 1) * per_iter
        sweep([base + s for s in desc], [None] * per_iter)
        return carry

    lax.fori_loop(0, i // groups, body, 0)

    o = jnp.where(first, acc_ref[0], acc_ref[1])
    ss = o * o
    s0 = jnp.sum(jnp.where(first, ss, 0.0), axis=-1, keepdims=True)
    s1 = jnp.sum(jnp.where(first, 0.0, ss), axis=-1, keepdims=True)
    ms = jnp.where(first, s0, s1) * (1.0 / hd)
    o_ref[...] = (o * lax.rsqrt(ms + EPS) * g_ref[...]).astype(BF16)


def _sb_prompt(q, k, v, bias, g, *, hd):
    b, t, w = q.shape
    kblk = SB_BLOCK if t % SB_BLOCK == 0 else t
    blk = SB_QUERY_BLOCKS * kblk if t % (SB_QUERY_BLOCKS * kblk) == 0 else kblk
    n_pairs = w // LANES
    return pl.pallas_call(
        functools.partial(_sb_prompt_kernel, hd=hd, kblk=kblk),
        grid=(b, n_pairs, t // blk),
        in_specs=[pl.BlockSpec(memory_space=pltpu.SMEM),
                  pl.BlockSpec((None, blk, LANES), lambda bi, p, i: (bi, i, p)),
                  pl.BlockSpec((None, t, LANES), lambda bi, p, i: (bi, 0, p)),
                  pl.BlockSpec((None, t, LANES), lambda bi, p, i: (bi, 0, p)),
                  pl.BlockSpec((1, LANES), lambda bi, p, i: (0, p))],
        out_specs=pl.BlockSpec((None, blk, LANES), lambda bi, p, i: (bi, i, p)),
        out_shape=jax.ShapeDtypeStruct((b, t, w), BF16),
        scratch_shapes=[pltpu.VMEM((2, blk, LANES), F32), pltpu.VMEM((2, blk, 1), F32)],
        compiler_params=_params(("parallel", "parallel", "arbitrary")),
        name="sb_attention_prompt",
    )(bias, q, k, v, g)


def _sb_sample_kernel(pt_ref, bias_ref, q_ref, kn_ref, vn_ref, g_ref, *rest, nh, hd, pages):
    k_pages, v_pages = rest[:pages], rest[pages:2 * pages]
    o_ref, acc_ref, run_ref = rest[2 * pages:]
    del pt_ref
    ts, w = q_ref.shape[1:]
    page = k_pages[0].shape[1]
    rows = nh * ts
    j = pl.program_id(1)

    row_head = _idiv(_iota((rows, w), 0), ts)
    col_head = _idiv(_iota((rows, w), 1), hd)
    q = jnp.concatenate([q_ref[0].astype(F32)] * nh, axis=0)
    q_bd = jnp.where(row_head == col_head, q, 0.0).astype(BF16)
    rh = _idiv(_iota((rows, 1), 0), ts)
    bias = jnp.zeros((rows, 1), F32)
    for h in range(nh):
        bias = jnp.where(rh == h, bias_ref[h] * LOG2E, bias)

    def tri(n):
        return (_iota((n, n), 0) >= _iota((n, n), 1)).astype(BF16)

    @pl.when(j == 0)
    def _():
        pad = jnp.zeros((LANES - ts, w), F32)
        kn = jnp.concatenate([kn_ref[0].astype(F32), pad], axis=0).astype(BF16)
        vn = jnp.concatenate([vn_ref[0].astype(F32), pad], axis=0).astype(BF16)
        z = _dot_nt(q_bd, kn) + bias
        tok = _iota((rows, LANES), 0) - _idiv(_iota((rows, LANES), 0), ts) * ts
        valid = _iota((rows, LANES), 1) < tok
        a, run = _sb_weights(z, _sb_within(z, tri(LANES), valid), jnp.zeros((rows, 1), F32), valid)
        run_ref[...] = run
        acc_ref[...] = _dot(a, vn)

    blk = 2 * page
    tri2 = tri(blk)
    kt = jnp.concatenate([r[...] for r in k_pages], axis=1).astype(BF16)
    vt = jnp.concatenate([r[...] for r in v_pages], axis=1).astype(BF16)
    z = _dot(q_bd, kt) + bias
    n_blk = pages // 2
    zs = [z[:, m * blk:(m + 1) * blk] for m in range(n_blk)]
    ws = [_sb_within(zm, tri2, None) for zm in zs]
    run = run_ref[...]
    parts = [None] * n_blk
    for m in reversed(range(n_blk)):
        parts[m], run = _sb_weights(zs[m], ws[m], run, None)
    run_ref[...] = run
    acc_ref[...] += _dot_nt(jnp.concatenate(parts, axis=1), vt)

    @pl.when(j == pl.num_programs(1) - 1)
    def _():
        acc = jnp.where(row_head == col_head, acc_ref[...], 0.0)
        o = jnp.sum(acc.reshape(nh, ts, w), axis=0)
        same_head = (_idiv(_iota((w, w), 0), hd) == _idiv(_iota((w, w), 1), hd)).astype(BF16)
        ms = _dot_hp_exact_rhs(o * o, same_head) * (1.0 / hd)
        o_ref[0] = (o * lax.rsqrt(ms + EPS) * g_ref[...]).astype(BF16)


def _sb_sample(q, k_new, v_new, cache_k, cache_v, layer, page_table, bias, g, *, nh, hd):
    b, ts, w = q.shape
    n_pages = page_table.shape[1]
    page = cache_k.shape[3]
    pages = PAGES_PER_STEP if n_pages % PAGES_PER_STEP == 0 else 2
    assert n_pages % pages == 0 and pages % 2 == 0
    steps = n_pages // pages

    def tok_spec():
        return pl.BlockSpec((1, ts, w), lambda i, j, pt: (i, 0, 0))

    def page_spec(r):
        return pl.BlockSpec((None, None, w, page),
                            lambda i, j, pt: (layer, pt[i, (steps - 1 - j) * pages + r], 0, 0))

    grid_spec = pltpu.PrefetchScalarGridSpec(
        num_scalar_prefetch=1,
        grid=(b, steps),
        in_specs=[pl.BlockSpec(memory_space=pltpu.SMEM), tok_spec(), tok_spec(), tok_spec(),
                  pl.BlockSpec(g.shape, lambda i, j, pt: (0, 0))]
                 + [page_spec(r) for r in range(pages)] * 2,
        out_specs=tok_spec(),
        scratch_shapes=[pltpu.VMEM((nh * ts, w), F32), pltpu.VMEM((nh * ts, 1), F32)],
    )
    return pl.pallas_call(
        functools.partial(_sb_sample_kernel, nh=nh, hd=hd, pages=pages),
        grid_spec=grid_spec,
        out_shape=jax.ShapeDtypeStruct((b, ts, w), BF16),
        compiler_params=_params(("parallel", "arbitrary")),
        name="sb_attention_paged",
    )(page_table, bias, q, k_new, v_new, g, *([cache_k] * pages), *([cache_v] * pages))


def _outproj_kernel(x_ref, og_ref, os_ref, w_ref, gate_ref, shift_ref, scale_ref, g_ref, *rest):
    tb, ts, d = x_ref.shape
    gw = og_ref.shape[1]
    mixed = _dot(og_ref[...].astype(BF16), w_ref[0:gw, :]) + _dot(os_ref[...], w_ref[gw:, :])
    x1 = x_ref[...] + gate_ref[...] * mixed.reshape(tb, ts, d)
    h = _modulated_norm(x1, g_ref[...], scale_ref[...], shift_ref[...]).reshape(tb * ts, d)
    if len(rest) == 4:
        wr_ref, x1_ref, h_ref, lg_ref = rest
        lg_ref[...] = _dot_hp(h, wr_ref[...])
    else:
        x1_ref, h_ref = rest
    x1_ref[...] = x1
    h_ref[...] = h.astype(BF16)


def _output_projection(x, o_gla, o_sb, w_out, gate, shift, scale, g, w_router=None):
    b, t, d = x.shape
    n = b * t
    tb, ts = _row_tiling(b, t, ROW_TILE // 2)
    tm, nt = tb * ts, t // ts
    x_spec = pl.BlockSpec((tb, ts, d), lambda i, j: (i, j, 0))
    m_spec = pl.BlockSpec((tb, 1, d), lambda i, j: (i, 0, 0))

    def rows(width):
        return pl.BlockSpec((tm, width), lambda i, j: (i * nt + j, 0))

    def full(a):
        return pl.BlockSpec(a.shape, lambda i, j: (0,) * a.ndim)

    in_specs = [x_spec, rows(o_gla.shape[1]), rows(o_sb.shape[1]), full(w_out), m_spec, m_spec, m_spec, full(g)]
    args = [x, o_gla, o_sb, w_out, gate, shift, scale, g]
    out_specs = [x_spec, rows(d)]
    out_shape = [jax.ShapeDtypeStruct((b, t, d), F32), jax.ShapeDtypeStruct((n, d), BF16)]
    if w_router is not None:
        in_specs.append(full(w_router))
        args.append(w_router)
        out_specs.append(rows(LANES))
        out_shape.append(jax.ShapeDtypeStruct((n, LANES), F32))
    return pl.pallas_call(
        _outproj_kernel,
        grid=(b // tb, nt),
        in_specs=in_specs, out_specs=out_specs, out_shape=out_shape,
        compiler_params=_params(("parallel", "parallel")),
        name="output_projection_norm",
    )(*args)


def _ffn_kernel(h_ref, wg_ref, wv_ref, wd_ref, x_ref, gate_ref, o_ref, acc_ref):
    tb, ts, d = x_ref.shape
    f = pl.program_id(2)

    @pl.when(f == 0)
    def _():
        acc_ref[...] = jnp.zeros_like(acc_ref)

    h = h_ref[...]
    act = _silu(_dot(h, wg_ref[...])) * _dot(h, wv_ref[...])
    acc_ref[...] += _dot(act.astype(BF16), wd_ref[...])

    @pl.when(f == pl.num_programs(2) - 1)
    def _():
        o_ref[...] = x_ref[...] + gate_ref[...] * acc_ref[...].reshape(tb, ts, d)


def _ffn_tile(dff, cap=512):
    for tf in range(cap - cap % LANES, 0, -LANES):
        if dff % tf == 0:
            return tf
    raise ValueError(dff)


def _dense_ffn(x, h, w_up, w_down, gate):
    b, t, d = x.shape
    dff = w_down.shape[0]
    tf = _ffn_tile(dff)
    nf = dff // tf
    tb, ts = _row_tiling(b, t, ROW_TILE)
    tm, nt = tb * ts, t // ts
    x_spec = pl.BlockSpec((tb, ts, d), lambda i, j, f: (i, j, 0))
    return pl.pallas_call(
        _ffn_kernel,
        grid=(b // tb, nt, nf),
        in_specs=[pl.BlockSpec((tm, d), lambda i, j, f: (i * nt + j, 0)),
                  pl.BlockSpec((d, tf), lambda i, j, f: (0, f)),
                  pl.BlockSpec((d, tf), lambda i, j, f: (0, nf + f)),
                  pl.BlockSpec((tf, d), lambda i, j, f: (f, 0)),
                  x_spec,
                  pl.BlockSpec((tb, 1, d), lambda i, j, f: (i, 0, 0))],
        out_specs=x_spec,
        out_shape=jax.ShapeDtypeStruct((b, t, d), F32),
        scratch_shapes=[pltpu.VMEM((tm, d), F32)],
        compiler_params=_params(("parallel", "parallel", "arbitrary")),
        name="swiglu_ffn",
    )(h, w_up, w_up, w_down, x, gate)


def _top2_gates(logits, n_experts):
    lane = _iota(logits.shape, 1).astype(F32)
    neg = jnp.float32(-jnp.inf)
    l1 = jnp.where(lane < n_experts, logits, neg)
    m1 = jnp.max(l1, axis=-1, keepdims=True)
    i1 = jnp.min(jnp.where(l1 == m1, lane, float(LANES)), axis=-1, keepdims=True)
    l2 = jnp.where(lane == i1, neg, l1)
    m2 = jnp.max(l2, axis=-1, keepdims=True)
    i2 = jnp.min(jnp.where(l2 == m2, lane, float(LANES)), axis=-1, keepdims=True)
    p2 = jnp.exp(m2 - m1)
    w1 = 1.0 / (1.0 + p2)
    return jnp.where(lane == i1, w1, 0.0) + jnp.where(lane == i2, p2 * w1, 0.0)


def _moe_kernel(h_ref, lg_ref, wg_ref, wv_ref, wd_ref, x_ref, gate_ref, o_ref,
                acc_ref, gates_ref, pos_ref, pos_t_ref, xg_ref, y_ref, count_ref, *, n_experts, rows):
    tb, ts, d = x_ref.shape
    tm = tb * ts
    e = pl.program_id(2)
    f = pl.program_id(3)
    last_f = pl.num_programs(3) - 1

    @pl.when((e == 0) & (f == 0))
    def _():
        acc_ref[...] = jnp.zeros_like(acc_ref)
        gates = _top2_gates(lg_ref[...], n_experts)
        gates_ref[...] = gates
        routed = gates != 0.0
        before = (_iota((tm, tm), 0) > _iota((tm, tm), 1)).astype(BF16)
        pos = jnp.where(routed, _dot(before, jnp.where(routed, 1.0, 0.0).astype(BF16)), -1.0)
        pos_ref[...] = pos
        pos_t = pos.T
        pos_t_ref[...] = pos_t
        for ee in range(n_experts):
            count_ref[ee] = (jnp.max(pos_t[ee:ee + 1, :]) + 1.0).astype(jnp.int32)

    pos_row = pos_t_ref[pl.ds(e, 1), :]
    n_chunks = _idiv(count_ref[e] + (rows - 1), rows)

    @pl.when(f == 0)
    def _():
        def gather(c, carry):
            r0 = pl.multiple_of(c * rows, rows)
            slot = (r0 + _iota((rows, 1), 0)).astype(F32)
            sel = jnp.where(pos_row == slot, 1.0, 0.0).astype(BF16)
            xg_ref[pl.ds(r0, rows), :] = _dot(sel, h_ref[...]).astype(BF16)
            y_ref[pl.ds(r0, rows), :] = jnp.zeros((rows, d), F32)
            return carry

        lax.fori_loop(0, n_chunks, gather, 0)

    def expert(c, carry):
        r0 = pl.multiple_of(c * rows, rows)
        xc = xg_ref[pl.ds(r0, rows), :]
        act = _silu(_dot(xc, wg_ref[0])) * _dot(xc, wv_ref[0])
        y_ref[pl.ds(r0, rows), :] += _dot(act.astype(BF16), wd_ref[0])
        return carry

    lax.fori_loop(0, n_chunks, expert, 0)

    @pl.when(f == last_f)
    def _():
        lane = _iota((tm, LANES), 1)
        pick = lane == e
        ge = jnp.sum(jnp.where(pick, gates_ref[...], 0.0), axis=-1, keepdims=True)
        pos_col = jnp.sum(jnp.where(pick, pos_ref[...], 0.0), axis=-1, keepdims=True)

        def scatter(c, carry):
            r0 = pl.multiple_of(c * rows, rows)
            slot = (r0 + _iota((1, rows), 1)).astype(F32)
            sel_t = jnp.where(pos_col == slot, 1.0, 0.0).astype(BF16)
            y_hi, y_lo = _split(y_ref[pl.ds(r0, rows), :])
            acc_ref[...] += ge * (_dot(sel_t, y_hi) + _dot(sel_t, y_lo))
            return carry

        lax.fori_loop(0, n_chunks, scatter, 0)

    @pl.when((e == n_experts - 1) & (f == last_f))
    def _():
        o_ref[...] = x_ref[...] + gate_ref[...] * acc_ref[...].reshape(tb, ts, d)


def _moe_ffn(x, h, logits, w_up, w_down, gate):
    b, t, d = x.shape
    n_experts, dff = w_down.shape[:2]
    tf = _ffn_tile(dff, cap=1024)
    nf = dff // tf
    tb, ts = _row_tiling(b, t, ROW_TILE)
    tm, nt = tb * ts, t // ts
    x_spec = pl.BlockSpec((tb, ts, d), lambda i, j, e, f: (i, j, 0))
    rows = min(MOE_CHUNK_ROWS, tm)
    assert tm % rows == 0
    return pl.pallas_call(
        functools.partial(_moe_kernel, n_experts=n_experts, rows=rows),
        grid=(b // tb, nt, n_experts, nf),
        in_specs=[pl.BlockSpec((tm, d), lambda i, j, e, f: (i * nt + j, 0)),
                  pl.BlockSpec((tm, LANES), lambda i, j, e, f: (i * nt + j, 0)),
                  pl.BlockSpec((1, d, tf), lambda i, j, e, f: (e, 0, f)),
                  pl.BlockSpec((1, d, tf), lambda i, j, e, f: (e, 0, nf + f)),
                  pl.BlockSpec((1, tf, d), lambda i, j, e, f: (e, f, 0)),
                  x_spec,
                  pl.BlockSpec((tb, 1, d), lambda i, j, e, f: (i, 0, 0))],
        out_specs=x_spec,
        out_shape=jax.ShapeDtypeStruct((b, t, d), F32),
        scratch_shapes=[pltpu.VMEM((tm, d), F32), pltpu.VMEM((tm, LANES), F32),
                        pltpu.VMEM((tm, LANES), F32), pltpu.VMEM((LANES, tm), F32),
                        pltpu.VMEM((tm, d), BF16), pltpu.VMEM((tm, d), F32),
                        pltpu.SMEM((n_experts,), jnp.int32)],
        compiler_params=_params(("parallel", "parallel", "arbitrary", "arbitrary")),
        name="moe_swiglu_ffn",
    )(h, logits, w_up, w_up, w_down, x, gate)


def _final_norm_kernel(x_ref, g_ref, o_ref):
    x = x_ref[...]
    ms = jnp.mean(x * x, axis=-1, keepdims=True)
    o_ref[...] = x * lax.rsqrt(ms + EPS) * g_ref[...]


def _final_norm(x, g):
    b, t, d = x.shape
    tb, ts = _row_tiling(b, t, ROW_TILE)
    x_spec = pl.BlockSpec((tb, ts, d), lambda i, j: (i, j, 0))
    return pl.pallas_call(
        _final_norm_kernel,
        grid=(b // tb, t // ts),
        in_specs=[x_spec, pl.BlockSpec(g.shape, lambda i, j: (0, 0, 0))],
        out_specs=x_spec,
        out_shape=jax.ShapeDtypeStruct((b, t, d), F32),
        compiler_params=_params(("parallel", "parallel")),
        name="final_rmsnorm",
    )(x, g)


def _arrange_w_in(w, widths):
    pieces, a = [], 0
    for wd in widths:
        pieces.append(w[:, a:a + wd])
        a += wd
    rank = pieces.pop(4)
    pieces.append(jnp.pad(rank, ((0, 0), (0, LANES - rank.shape[1]))))
    return jnp.concatenate(pieces, axis=1).astype(BF16)


def kernel(x_prompt, x_sample, c_prompt, c_sample, cache_sb_k, cache_sb_v, state_gla, page_table, w_ada, b_ada, norm_attn, norm_ffn, w_in, w_gate_up, b_gate, norm_gla, sb_bias, norm_sb, w_out, w_ff_up, w_ff_down, w_router, w_exp_up, w_exp_down, norm_final):
    depth, d = norm_attn.shape
    _, db, nh_gla, dk, dv = state_gla.shape
    _, n_phys, page, nh_sb, hd = cache_sb_k.shape
    rank, qk = w_gate_up.shape[1:]
    gw, sw = nh_gla * dv, nh_sb * hd
    widths = (qk, qk, gw, gw, rank, sw, sw, sw)
    bp, tp, _ = x_prompt.shape
    _, tsm, _ = x_sample.shape
    n_experts = w_router.shape[-1]

    mod = _modulation(jnp.concatenate([c_prompt, c_sample], axis=0), w_ada, b_ada)
    cache_k = cache_sb_k.transpose(0, 1, 3, 4, 2).reshape(depth, n_phys, sw, page)
    cache_v = cache_sb_v.transpose(0, 1, 3, 4, 2).reshape(depth, n_phys, sw, page)
    gla_zero = jnp.zeros((bp, nh_gla, dk, dv), F32)

    groups = {
        "prompt": dict(x=x_prompt, rows=slice(0, bp), s0=lambda l: gla_zero),
        "sample": dict(x=x_sample, rows=slice(bp, bp + db), s0=lambda l: state_gla[l]),
    }
    collected = {name: dict(k=[], v=[], s=[]) for name in groups}

    for l in range(depth):
        w_in_l = _arrange_w_in(w_in[l], widths)
        wg_l = jnp.pad(w_gate_up[l], ((0, LANES - rank), (0, 0)))
        bg_l = b_gate[l].reshape(1, qk)
        w_out_l = w_out[l].astype(BF16)
        moe = l % 2 == 1
        if moe:
            w_r = jnp.pad(w_router[l // 2], ((0, 0), (0, LANES - n_experts)))
            w_up_l = w_exp_up[l // 2].astype(BF16)
            w_down_l = w_exp_down[l // 2].astype(BF16)
        else:
            w_up_l = w_ff_up[l // 2].astype(BF16)
            w_down_l = w_ff_down[l // 2].astype(BF16)
        for name, grp in groups.items():
            x = grp["x"]
            b, t, _ = x.shape
            m = [mod[l, grp["rows"], i * d:(i + 1) * d].reshape(b, 1, d) for i in range(6)]
            shift1, scale1, gate1, shift2, scale2, gate2 = m
            gq, gk, gv, gg, la, sq, sk, sv, skb, svb = _input_projection(
                x, shift1, scale1, norm_attn[l].reshape(1, 1, d), w_in_l, wg_l, bg_l,
                qk=qk, gw=gw, sw=sw, dk=dk, hd=hd)
            o_gla, s_new = _gla(gq.reshape(b, t, qk), gk.reshape(b, t, qk), gv.reshape(b, t, gw),
                                la.reshape(b, t, qk), gg.reshape(b, t, gw), grp["s0"](l),
                                norm_gla[l].reshape(1, gw))
            g_sb = norm_sb[l].reshape(1, sw)
            if name == "prompt":
                o_sb = _sb_prompt(sq.reshape(b, t, sw), skb.reshape(b, t, sw), svb.reshape(b, t, sw),
                                  sb_bias[l], g_sb, hd=hd)
            else:
                o_sb = _sb_sample(sq.reshape(b, t, sw), skb.reshape(b, t, sw), svb.reshape(b, t, sw),
                                  cache_k, cache_v, l, page_table, sb_bias[l], g_sb, nh=nh_sb, hd=hd)
            res = _output_projection(x, o_gla.reshape(b * t, gw), o_sb.reshape(b * t, sw), w_out_l,
                                     gate1, shift2, scale2, norm_ffn[l].reshape(1, 1, d),
                                     w_r if moe else None)
            if moe:
                x1, h2, logits = res
                x = _moe_ffn(x1, h2, logits, w_up_l, w_down_l, gate2)
            else:
                x1, h2 = res
                x = _dense_ffn(x1, h2, w_up_l, w_down_l, gate2)
            grp["x"] = x
            collected[name]["k"].append(sk.reshape(b, t, nh_sb, hd))
            collected[name]["v"].append(sv.reshape(b, t, nh_sb, hd))
            collected[name]["s"].append(s_new)

    g_final = norm_final.reshape(1, 1, d)
    y_prompt = _final_norm(groups["prompt"]["x"], g_final)
    y_sample = _final_norm(groups["sample"]["x"], g_final)
    cp, cs = collected["prompt"], collected["sample"]
    return (y_prompt, y_sample, jnp.stack(cp["k"]), jnp.stack(cp["v"]), jnp.stack(cp["s"]),
            jnp.stack(cs["k"]), jnp.stack(cs["v"]), jnp.stack(cs["s"]))
```

```python
import functools

import jax
import jax.numpy as jnp
from jax import lax
from jax.experimental import pallas as pl
from jax.experimental.pallas import tpu as pltpu

F32 = jnp.float32
BF16 = jnp.bfloat16
EPS = 1e-6
GATE_TAU = 16.0
GLA_CHUNK = 64
GLA_SUB = 16
TOP_K = 2
LANES = 128
SB_BLOCK = 256
SB_QUERY_BLOCKS = 2
SB_SWEEP_BLOCKS = 4
PAGES_PER_STEP = 16
LOG2E = 1.4426950408889634
SOFTPLUS2_CLAMP = 126.0
ROW_TILE = 1024
MOE_CHUNK_ROWS = 288
VMEM_LIMIT = 56 * 1024 * 1024


def _params(sem):
    return pltpu.CompilerParams(dimension_semantics=sem, vmem_limit_bytes=VMEM_LIMIT)


def _dot(a, b):
    return jnp.dot(a, b, preferred_element_type=F32)


def _dot_nt(a, b):
    return lax.dot_general(a, b, (((1,), (1,)), ((), ())), preferred_element_type=F32)


def _dot_tn(a, b):
    return lax.dot_general(a, b, (((0,), (0,)), ((), ())), preferred_element_type=F32)


def _split(x):
    hi = x.astype(BF16)
    return hi, (x - hi.astype(F32)).astype(BF16)


def _dot_hp(a, b):
    ah, al = _split(a)
    bh, bl = _split(b)
    return _dot(ah, bh) + (_dot(ah, bl) + _dot(al, bh))


def _dot_hp_exact_rhs(a, b_bf16):
    ah, al = _split(a)
    return _dot(ah, b_bf16) + _dot(al, b_bf16)


def _silu(x):
    return x * (1.0 / (1.0 + jnp.exp(-x)))


def _log_sigmoid(x):
    return jnp.minimum(x, 0.0) - jnp.log1p(jnp.exp(-jnp.abs(x)))


def _idiv(x, n):
    if n & (n - 1) == 0:
        return lax.shift_right_logical(x, n.bit_length() - 1)
    return x // n


def _iota(shape, dim):
    return lax.broadcasted_iota(jnp.int32, shape, dim)


def _modulated_norm(x, g, scale, shift):
    ms = jnp.mean(x * x, axis=-1, keepdims=True)
    return (x * lax.rsqrt(ms + EPS) * g) * (1.0 + scale) + shift


def _row_tiling(b, t, target):
    if t >= target:
        assert t % target == 0
        return 1, target
    tb = min(b, max(1, target // t))
    assert b % tb == 0
    return tb, t


def _mod_kernel(c_ref, w_ref, b_ref, o_ref):
    o_ref[0] = _dot_hp(_silu(c_ref[...]), w_ref[0]) + b_ref[0]


def _modulation(c, w_ada, b_ada):
    depth, d, n = w_ada.shape
    bc = c.shape[0]
    tn = n // 4
    return pl.pallas_call(
        _mod_kernel,
        grid=(depth, n // tn),
        in_specs=[pl.BlockSpec((bc, d), lambda l, j: (0, 0)),
                  pl.BlockSpec((1, d, tn), lambda l, j: (l, 0, j)),
                  pl.BlockSpec((1, 1, tn), lambda l, j: (l, 0, j))],
        out_specs=pl.BlockSpec((1, bc, tn), lambda l, j: (l, 0, j)),
        out_shape=jax.ShapeDtypeStruct((depth, bc, n), F32),
        compiler_params=_params(("parallel", "parallel")),
        name="adaln_modulation",
    )(c, w_ada, b_ada.reshape(depth, 1, n))


def _inproj_kernel(x_ref, shift_ref, scale_ref, g_ref, w_ref, wg_ref, bg_ref,
                   gq_ref, gk_ref, gv_ref, gg_ref, la_ref, sq_ref, sk_ref, sv_ref, skb_ref, svb_ref,
                   *, qk, gw, sw, q_scale_gla, q_scale_sb):
    tb, ts, d = x_ref.shape
    h = _modulated_norm(x_ref[...], g_ref[...], scale_ref[...], shift_ref[...])
    hb = h.reshape(tb * ts, d).astype(BF16)
    off = [0]

    def proj(width):
        a = off[0]
        off[0] = a + width
        return _dot(hb, w_ref[:, a:a + width])

    gq_ref[...] = proj(qk) * q_scale_gla
    gk_ref[...] = proj(qk)
    gv_ref[...] = proj(gw)
    gg_ref[...] = proj(gw)
    sq_ref[...] = (proj(sw) * q_scale_sb).astype(BF16)
    sk = proj(sw)
    sk_ref[...] = sk
    skb_ref[...] = sk.astype(BF16)
    sv = proj(sw)
    sv_ref[...] = sv
    svb_ref[...] = sv.astype(BF16)
    ga = proj(LANES)
    gate = _dot_hp(ga, wg_ref[...]) + bg_ref[...]
    la_ref[...] = _log_sigmoid(gate) * (1.0 / GATE_TAU)


def _input_projection(x, shift, scale, g, w, wg, bg, *, qk, gw, sw, dk, hd):
    b, t, d = x.shape
    n = b * t
    tb, ts = _row_tiling(b, t, ROW_TILE // 2)
    tm = tb * ts
    nt = t // ts
    x_spec = pl.BlockSpec((tb, ts, d), lambda i, j: (i, j, 0))
    m_spec = pl.BlockSpec((tb, 1, d), lambda i, j: (i, 0, 0))

    def full(a):
        return pl.BlockSpec(a.shape, lambda i, j: (0,) * a.ndim)

    def out(width, dtype):
        return (pl.BlockSpec((tm, width), lambda i, j: (i * nt + j, 0)),
                jax.ShapeDtypeStruct((n, width), dtype))

    outs = [out(qk, F32), out(qk, F32), out(gw, F32), out(gw, F32), out(qk, F32),
            out(sw, BF16), out(sw, F32), out(sw, F32), out(sw, BF16), out(sw, BF16)]
    kern = functools.partial(_inproj_kernel, qk=qk, gw=gw, sw=sw,
                             q_scale_gla=dk ** -0.5, q_scale_sb=hd ** -0.5 * LOG2E)
    return pl.pallas_call(
        kern,
        grid=(b // tb, nt),
        in_specs=[x_spec, m_spec, m_spec, full(g), full(w), full(wg), full(bg)],
        out_specs=[o[0] for o in outs],
        out_shape=[o[1] for o in outs],
        compiler_params=_params(("parallel", "parallel")),
        name="norm_input_projection",
    )(x, shift, scale, g, w, wg, bg)


def _gla_kernel(q_ref, k_ref, v_ref, la_ref, gg_ref, s0_ref, g_ref, o_ref, s_out_ref, st_ref,
                *, chunk, sub):
    _, nh, dk, dv = s0_ref.shape
    tblock = q_ref.shape[1]
    qk, vw = nh * dk, nh * dv
    t = pl.program_id(1)

    @pl.when(t == 0)
    def _():
        rows = []
        for h in range(nh):
            rows.append(jnp.concatenate(
                [s0_ref[0, h] if hh == h else jnp.zeros((dk, dv), F32) for hh in range(nh)], axis=1))
        st_ref[...] = jnp.concatenate(rows, axis=0).T

    state_mask = _idiv(_iota((vw, qk), 0), dv) == _idiv(_iota((vw, qk), 1), dk)
    head_expand = (_idiv(_iota((qk, vw), 0), dk) == _idiv(_iota((qk, vw), 1), dv)).astype(BF16)
    key_mask = _idiv(_iota((nh * sub, qk), 0), sub) == _idiv(_iota((nh * sub, qk), 1), dk)
    val_mask = _idiv(_iota((nh * sub, vw), 0), sub) == _idiv(_iota((nh * sub, vw), 1), dv)
    tril = (_iota((chunk, chunk), 0) >= _iota((chunk, chunk), 1)).astype(BF16)
    row_id = _iota((chunk, 1), 0)
    pair_mask = _iota((sub, sub, 1), 0) >= _iota((sub, sub, 1), 1)
    n_sub = chunk // sub

    def one_chunk(ci, carry):
        r = pl.multiple_of(ci * chunk, chunk)
        q = q_ref[0, pl.ds(r, chunk), :]
        k = k_ref[0, pl.ds(r, chunk), :]
        v = v_ref[0, pl.ds(r, chunk), :]
        la_hi, la_lo = _split(la_ref[0, pl.ds(r, chunk), :])
        cum = _dot(tril, la_hi) + _dot(tril, la_lo)
        last = cum[chunk - 1:chunk, :]
        st = st_ref[...]

        o = _dot_nt((q * jnp.exp(cum)).astype(BF16), st.astype(BF16))
        k_dec = (k * jnp.exp(last - cum)).astype(BF16)
        upd = _dot_tn(v.astype(BF16), k_dec)
        st_ref[...] = st * jnp.exp(last) + jnp.where(state_mask, upd, 0.0)

        for j in range(n_sub - 1):
            a = j * sub
            ref_row = cum[a + sub - 1:a + sub, :]
            q_dec = (q * jnp.exp(jnp.minimum(cum - ref_row, 0.0))).astype(BF16)
            k_j = k[a:a + sub] * jnp.exp(ref_row - cum[a:a + sub])
            k_bd = jnp.where(key_mask, jnp.concatenate([k_j] * nh, axis=0), 0.0).astype(BF16)
            sc = _dot_nt(q_dec, k_bd)
            sc = jnp.where(row_id >= a + sub, sc, 0.0).astype(BF16)
            v_bd = jnp.where(val_mask, jnp.concatenate([v[a:a + sub]] * nh, axis=0), 0.0).astype(BF16)
            o = o + _dot(sc, v_bd)

        diag = []
        for j in range(n_sub):
            a = j * sub
            l_j, q_j, k_j, v_j = cum[a:a + sub], q[a:a + sub], k[a:a + sub], v[a:a + sub]
            dec = jnp.exp(jnp.minimum(l_j[:, None, :] - l_j[None, :, :], 0.0))
            p = jnp.where(pair_mask, (q_j[:, None, :] * k_j[None, :, :]) * dec, 0.0)
            sc = _dot(p.reshape(sub * sub, qk).astype(BF16), head_expand)
            diag.append(jnp.sum(sc.reshape(sub, sub, vw) * v_j[None, :, :], axis=1))
        o = o + (diag[0] if n_sub == 1 else jnp.concatenate(diag, axis=0))

        g = g_ref[...]
        gg = gg_ref[0, pl.ds(r, chunk), :]
        outs = []
        for h in range(nh):
            oh = o[:, h * dv:(h + 1) * dv]
            ms = jnp.mean(oh * oh, axis=-1, keepdims=True)
            outs.append(oh * lax.rsqrt(ms + EPS) * g[:, h * dv:(h + 1) * dv])
        o_ref[0, pl.ds(r, chunk), :] = jnp.concatenate(outs, axis=1) * _silu(gg)
        return carry

    lax.fori_loop(0, tblock // chunk, one_chunk, 0)

    @pl.when(t == pl.num_programs(1) - 1)
    def _():
        s_bd = st_ref[...].T
        for h in range(nh):
            s_out_ref[0, h] = s_bd[h * dk:(h + 1) * dk, h * dv:(h + 1) * dv]


def _gla(q, k, v, la, gg, s0, g):
    b, t, qk = q.shape
    vw = v.shape[-1]
    chunk = GLA_CHUNK if t % GLA_CHUNK == 0 else t
    sub = GLA_SUB if chunk % GLA_SUB == 0 else chunk
    tblock = min(t, 8 * chunk)
    assert t % tblock == 0

    def tok(width):
        return pl.BlockSpec((1, tblock, width), lambda i, j: (i, j, 0))

    s_spec = pl.BlockSpec((1,) + s0.shape[1:], lambda i, j: (i, 0, 0, 0))
    return pl.pallas_call(
        functools.partial(_gla_kernel, chunk=chunk, sub=sub),
        grid=(b, t // tblock),
        in_specs=[tok(qk), tok(qk), tok(vw), tok(qk), tok(vw), s_spec,
                  pl.BlockSpec(g.shape, lambda i, j: (0, 0))],
        out_specs=[tok(vw), s_spec],
        out_shape=[jax.ShapeDtypeStruct((b, t, vw), F32), jax.ShapeDtypeStruct(s0.shape, F32)],
        scratch_shapes=[pltpu.VMEM((vw, qk), F32)],
        compiler_params=_params(("parallel", "arbitrary")),
        name="gla_chunked",
    )(q, k, v, la, gg, s0, g)


def _softplus2(z2):
    return jnp.maximum(z2, jnp.log2(1.0 + jnp.exp2(jnp.minimum(z2, SOFTPLUS2_CLAMP))))


def _sb_within(z2, tri, valid):
    sp = _softplus2(z2)
    if valid is not None:
        sp = jnp.where(valid, sp, 0.0)
    return _dot(sp.astype(BF16), tri)


def _sb_weights(z2, within, run, valid):
    a = jnp.exp2(z2 - (within + run))
    if valid is not None:
        a = jnp.where(valid, a, 0.0)
    return a.astype(BF16), run + within[:, 0:1]


def _sb_prompt_kernel(bias_ref, q_ref, k_ref, v_ref, g_ref, o_ref, acc_ref, run_ref, *, hd, kblk):
    qblk = q_ref.shape[0]
    n_sub = qblk // kblk
    p = pl.program_id(1)
    i = pl.program_id(2)
    q = q_ref[...]
    first = _iota((1, LANES), 1) < hd
    zero = jnp.zeros_like(q)
    q_heads = (jnp.where(first, q, zero), jnp.where(first, zero, q))
    biases = (bias_ref[2 * p] * LOG2E, bias_ref[2 * p + 1] * LOG2E)
    tri = (_iota((kblk, kblk), 0) >= _iota((kblk, kblk), 1)).astype(BF16)
    acc_ref[...] = jnp.zeros_like(acc_ref)
    run_ref[...] = jnp.zeros_like(run_ref)

    def sweep(blocks, valids):
        ks = [k_ref[pl.ds(pl.multiple_of(jb * kblk, kblk), kblk), :] for jb in blocks]
        v_all = v_ref[pl.ds(pl.multiple_of(blocks[-1] * kblk, kblk), len(blocks) * kblk), :]
        for h in range(2):
            zs = [_dot_nt(q_heads[h], kb) + biases[h] for kb in ks]
            ws = [_sb_within(z, tri, valid) for z, valid in zip(zs, valids)]
            run = run_ref[h]
            parts = []
            for z, w, valid in zip(zs, ws, valids):
                a, run = _sb_weights(z, w, run, valid)
                parts.append(a)
            run_ref[h] = run
            acc_ref[h] += _dot(jnp.concatenate(parts[::-1], axis=1), v_all)

    row = _iota((qblk, kblk), 0)
    col = _iota((qblk, kblk), 1)
    diag = list(reversed(range(n_sub)))
    sweep([i * n_sub + s for s in diag], [col + s * kblk < row for s in diag])

    groups = SB_SWEEP_BLOCKS // n_sub if SB_SWEEP_BLOCKS % n_sub == 0 else 1
    per_iter = groups * n_sub
    odd = lax.rem(i, groups)
    for r in range(1, groups):
        @pl.when(odd >= r)
        def _():
            base = (i - r) * n_sub
            sweep([base + s for s in diag], [None] * n_sub)
    top = (i - odd) * n_sub
    desc = list(reversed(range(per_iter)))

    def body(j, carry):
        base = top - (j + 1) * per_iter
        sweep([base + s for s in desc], [None] * per_iter)
        return carry

    lax.fori_loop(0, i // groups, body, 0)

    o = jnp.where(first, acc_ref[0], acc_ref[1])
    ss = o * o
    s0 = jnp.sum(jnp.where(first, ss, 0.0), axis=-1, keepdims=True)
    s1 = jnp.sum(jnp.where(first, 0.0, ss), axis=-1, keepdims=True)
    ms = jnp.where(first, s0, s1) * (1.0 / hd)
    o_ref[...] = (o * lax.rsqrt(ms + EPS) * g_ref[...]).astype(BF16)


def _sb_prompt(q, k, v, bias, g, *, hd):
    b, t, w = q.shape
    kblk = SB_BLOCK if t % SB_BLOCK == 0 else t
    blk = SB_QUERY_BLOCKS * kblk if t % (SB_QUERY_BLOCKS * kblk) == 0 else kblk
    n_pairs = w // LANES
    return pl.pallas_call(
        functools.partial(_sb_prompt_kernel, hd=hd, kblk=kblk),
        grid=(b, n_pairs, t // blk),
        in_specs=[pl.BlockSpec(memory_space=pltpu.SMEM),
                  pl.BlockSpec((None, blk, LANES), lambda bi, p, i: (bi, i, p)),
                  pl.BlockSpec((None, t, LANES), lambda bi, p, i: (bi, 0, p)),
                  pl.BlockSpec((None, t, LANES), lambda bi, p, i: (bi, 0, p)),
                  pl.BlockSpec((1, LANES), lambda bi, p, i: (0, p))],
        out_specs=pl.BlockSpec((None, blk, LANES), lambda bi, p, i: (bi, i, p)),
        out_shape=jax.ShapeDtypeStruct((b, t, w), BF16),
        scratch_shapes=[pltpu.VMEM((2, blk, LANES), F32), pltpu.VMEM((2, blk, 1), F32)],
        compiler_params=_params(("parallel", "parallel", "arbitrary")),
        name="sb_attention_prompt",
    )(bias, q, k, v, g)


def _sb_sample_kernel(pt_ref, bias_ref, q_ref, kn_ref, vn_ref, g_ref, *rest, nh, hd, pages):
    k_pages, v_pages = rest[:pages], rest[pages:2 * pages]
    o_ref, acc_ref, run_ref = rest[2 * pages:]
    del pt_ref
    ts, w = q_ref.shape[1:]
    page = k_pages[0].shape[1]
    rows = nh * ts
    j = pl.program_id(1)

    row_head = _idiv(_iota((rows, w), 0), ts)
    col_head = _idiv(_iota((rows, w), 1), hd)
    q = jnp.concatenate([q_ref[0].astype(F32)] * nh, axis=0)
    q_bd = jnp.where(row_head == col_head, q, 0.0).astype(BF16)
    rh = _idiv(_iota((rows, 1), 0), ts)
    bias = jnp.zeros((rows, 1), F32)
    for h in range(nh):
        bias = jnp.where(rh == h, bias_ref[h] * LOG2E, bias)

    def tri(n):
        return (_iota((n, n), 0) >= _iota((n, n), 1)).astype(BF16)

    @pl.when(j == 0)
    def _():
        pad = jnp.zeros((LANES - ts, w), F32)
        kn = jnp.concatenate([kn_ref[0].astype(F32), pad], axis=0).astype(BF16)
        vn = jnp.concatenate([vn_ref[0].astype(F32), pad], axis=0).astype(BF16)
        z = _dot_nt(q_bd, kn) + bias
        tok = _iota((rows, LANES), 0) - _idiv(_iota((rows, LANES), 0), ts) * ts
        valid = _iota((rows, LANES), 1) < tok
        a, run = _sb_weights(z, _sb_within(z, tri(LANES), valid), jnp.zeros((rows, 1), F32), valid)
        run_ref[...] = run
        acc_ref[...] = _dot(a, vn)

    blk = 2 * page
    tri2 = tri(blk)
    kt = jnp.concatenate([r[...] for r in k_pages], axis=1).astype(BF16)
    vt = jnp.concatenate([r[...] for r in v_pages], axis=1).astype(BF16)
    z = _dot(q_bd, kt) + bias
    n_blk = pages // 2
    zs = [z[:, m * blk:(m + 1) * blk] for m in range(n_blk)]
    ws = [_sb_within(zm, tri2, None) for zm in zs]
    run = run_ref[...]
    parts = [None] * n_blk
    for m in reversed(range(n_blk)):
        parts[m], run = _sb_weights(zs[m], ws[m], run, None)
    run_ref[...] = run
    acc_ref[...] += _dot_nt(jnp.concatenate(parts, axis=1), vt)

    @pl.when(j == pl.num_programs(1) - 1)
    def _():
        acc = jnp.where(row_head == col_head, acc_ref[...], 0.0)
        o = jnp.sum(acc.reshape(nh, ts, w), axis=0)
        same_head = (_idiv(_iota((w, w), 0), hd) == _idiv(_iota((w, w), 1), hd)).astype(BF16)
        ms = _dot_hp_exact_rhs(o * o, same_head) * (1.0 / hd)
        o_ref[0] = (o * lax.rsqrt(ms + EPS) * g_ref[...]).astype(BF16)


def _sb_sample(q, k_new, v_new, cache_k, cache_v, layer, page_table, bias, g, *, nh, hd):
    b, ts, w = q.shape
    n_pages = page_table.shape[1]
    page = cache_k.shape[3]
    pages = PAGES_PER_STEP if n_pages % PAGES_PER_STEP == 0 else 2
    assert n_pages % pages == 0 and pages % 2 == 0
    steps = n_pages // pages

    def tok_spec():
        return pl.BlockSpec((1, ts, w), lambda i, j, pt: (i, 0, 0))

    def page_spec(r):
        return pl.BlockSpec((None, None, w, page),
                            lambda i, j, pt: (layer, pt[i, (steps - 1 - j) * pages + r], 0, 0))

    grid_spec = pltpu.PrefetchScalarGridSpec(
        num_scalar_prefetch=1,
        grid=(b, steps),
        in_specs=[pl.BlockSpec(memory_space=pltpu.SMEM), tok_spec(), tok_spec(), tok_spec(),
                  pl.BlockSpec(g.shape, lambda i, j, pt: (0, 0))]
                 + [page_spec(r) for r in range(pages)] * 2,
        out_specs=tok_spec(),
        scratch_shapes=[pltpu.VMEM((nh * ts, w), F32), pltpu.VMEM((nh * ts, 1), F32)],
    )
    return pl.pallas_call(
        functools.partial(_sb_sample_kernel, nh=nh, hd=hd, pages=pages),
        grid_spec=grid_spec,
        out_shape=jax.ShapeDtypeStruct((b, ts, w), BF16),
        compiler_params=_params(("parallel", "arbitrary")),
        name="sb_attention_paged",
    )(page_table, bias, q, k_new, v_new, g, *([cache_k] * pages), *([cache_v] * pages))


def _outproj_kernel(x_ref, og_ref, os_ref, w_ref, gate_ref, shift_ref, scale_ref, g_ref, *rest):
    tb, ts, d = x_ref.shape
    gw = og_ref.shape[1]
    mixed = _dot(og_ref[...].astype(BF16), w_ref[0:gw, :]) + _dot(os_ref[...], w_ref[gw:, :])
    x1 = x_ref[...] + gate_ref[...] * mixed.reshape(tb, ts, d)
    h = _modulated_norm(x1, g_ref[...], scale_ref[...], shift_ref[...]).reshape(tb * ts, d)
    if len(rest) == 4:
        wr_ref, x1_ref, h_ref, lg_ref = rest
        lg_ref[...] = _dot_hp(h, wr_ref[...])
    else:
        x1_ref, h_ref = rest
    x1_ref[...] = x1
    h_ref[...] = h.astype(BF16)


def _output_projection(x, o_gla, o_sb, w_out, gate, shift, scale, g, w_router=None):
    b, t, d = x.shape
    n = b * t
    tb, ts = _row_tiling(b, t, ROW_TILE // 2)
    tm, nt = tb * ts, t // ts
    x_spec = pl.BlockSpec((tb, ts, d), lambda i, j: (i, j, 0))
    m_spec = pl.BlockSpec((tb, 1, d), lambda i, j: (i, 0, 0))

    def rows(width):
        return pl.BlockSpec((tm, width), lambda i, j: (i * nt + j, 0))

    def full(a):
        return pl.BlockSpec(a.shape, lambda i, j: (0,) * a.ndim)

    in_specs = [x_spec, rows(o_gla.shape[1]), rows(o_sb.shape[1]), full(w_out), m_spec, m_spec, m_spec, full(g)]
    args = [x, o_gla, o_sb, w_out, gate, shift, scale, g]
    out_specs = [x_spec, rows(d)]
    out_shape = [jax.ShapeDtypeStruct((b, t, d), F32), jax.ShapeDtypeStruct((n, d), BF16)]
    if w_router is not None:
        in_specs.append(full(w_router))
        args.append(w_router)
        out_specs.append(rows(LANES))
        out_shape.append(jax.ShapeDtypeStruct((n, LANES), F32))
    return pl.pallas_call(
        _outproj_kernel,
        grid=(b // tb, nt),
        in_specs=in_specs, out_specs=out_specs, out_shape=out_shape,
        compiler_params=_params(("parallel", "parallel")),
        name="output_projection_norm",
    )(*args)


def _ffn_kernel(h_ref, wg_ref, wv_ref, wd_ref, x_ref, gate_ref, o_ref, acc_ref):
    tb, ts, d = x_ref.shape
    f = pl.program_id(2)

    @pl.when(f == 0)
    def _():
        acc_ref[...] = jnp.zeros_like(acc_ref)

    h = h_ref[...]
    act = _silu(_dot(h, wg_ref[...])) * _dot(h, wv_ref[...])
    acc_ref[...] += _dot(act.astype(BF16), wd_ref[...])

    @pl.when(f == pl.num_programs(2) - 1)
    def _():
        o_ref[...] = x_ref[...] + gate_ref[...] * acc_ref[...].reshape(tb, ts, d)


def _ffn_tile(dff, cap=512):
    for tf in range(cap - cap % LANES, 0, -LANES):
        if dff % tf == 0:
            return tf
    raise ValueError(dff)


def _dense_ffn(x, h, w_up, w_down, gate):
    b, t, d = x.shape
    dff = w_down.shape[0]
    tf = _ffn_tile(dff)
    nf = dff // tf
    tb, ts = _row_tiling(b, t, ROW_TILE)
    tm, nt = tb * ts, t // ts
    x_spec = pl.BlockSpec((tb, ts, d), lambda i, j, f: (i, j, 0))
    return pl.pallas_call(
        _ffn_kernel,
        grid=(b // tb, nt, nf),
        in_specs=[pl.BlockSpec((tm, d), lambda i, j, f: (i * nt + j, 0)),
                  pl.BlockSpec((d, tf), lambda i, j, f: (0, f)),
                  pl.BlockSpec((d, tf), lambda i, j, f: (0, nf + f)),
                  pl.BlockSpec((tf, d), lambda i, j, f: (f, 0)),
                  x_spec,
                  pl.BlockSpec((tb, 1, d), lambda i, j, f: (i, 0, 0))],
        out_specs=x_spec,
        out_shape=jax.ShapeDtypeStruct((b, t, d), F32),
        scratch_shapes=[pltpu.VMEM((tm, d), F32)],
        compiler_params=_params(("parallel", "parallel", "arbitrary")),
        name="swiglu_ffn",
    )(h, w_up, w_up, w_down, x, gate)


def _top2_gates(logits, n_experts):
    lane = _iota(logits.shape, 1).astype(F32)
    neg = jnp.float32(-jnp.inf)
    l1 = jnp.where(lane < n_experts, logits, neg)
    m1 = jnp.max(l1, axis=-1, keepdims=True)
    i1 = jnp.min(jnp.where(l1 == m1, lane, float(LANES)), axis=-1, keepdims=True)
    l2 = jnp.where(lane == i1, neg, l1)
    m2 = jnp.max(l2, axis=-1, keepdims=True)
    i2 = jnp.min(jnp.where(l2 == m2, lane, float(LANES)), axis=-1, keepdims=True)
    p2 = jnp.exp(m2 - m1)
    w1 = 1.0 / (1.0 + p2)
    return jnp.where(lane == i1, w1, 0.0) + jnp.where(lane == i2, p2 * w1, 0.0)


def _moe_kernel(h_ref, lg_ref, wg_ref, wv_ref, wd_ref, x_ref, gate_ref, gf_ref, o_ref,
                acc_ref, gates_ref, pos_ref, pos_t_ref, xg_ref, y_ref, count_ref,
                *, n_experts, rows, final_norm):
    tb, ts, d = x_ref.shape
    tm = tb * ts
    e = pl.program_id(2)
    f = pl.program_id(3)
    last_f = pl.num_programs(3) - 1

    @pl.when((e == 0) & (f == 0))
    def _():
        acc_ref[...] = jnp.zeros_like(acc_ref)
        gates = _top2_gates(lg_ref[...], n_experts)
        gates_ref[...] = gates
        routed = gates != 0.0
        before = (_iota((tm, tm), 0) > _iota((tm, tm), 1)).astype(BF16)
        pos = jnp.where(routed, _dot(before, jnp.where(routed, 1.0, 0.0).astype(BF16)), -1.0)
        pos_ref[...] = pos
        pos_t = pos.T
        pos_t_ref[...] = pos_t
        for ee in range(n_experts):
            count_ref[ee] = (jnp.max(pos_t[ee:ee + 1, :]) + 1.0).astype(jnp.int32)

    pos_row = pos_t_ref[pl.ds(e, 1), :]
    n_chunks = _idiv(count_ref[e] + (rows - 1), rows)

    @pl.when(f == 0)
    def _():
        def gather(c, carry):
            r0 = pl.multiple_of(c * rows, rows)
            slot = (r0 + _iota((rows, 1), 0)).astype(F32)
            sel = jnp.where(pos_row == slot, 1.0, 0.0).astype(BF16)
            xg_ref[pl.ds(r0, rows), :] = _dot(sel, h_ref[...]).astype(BF16)
            y_ref[pl.ds(r0, rows), :] = jnp.zeros((rows, d), F32)
            return carry

        lax.fori_loop(0, n_chunks, gather, 0)

    def expert(c, carry):
        r0 = pl.multiple_of(c * rows, rows)
        xc = xg_ref[pl.ds(r0, rows), :]
        act = _silu(_dot(xc, wg_ref[0])) * _dot(xc, wv_ref[0])
        y_ref[pl.ds(r0, rows), :] += _dot(act.astype(BF16), wd_ref[0])
        return carry

    lax.fori_loop(0, n_chunks, expert, 0)

    @pl.when(f == last_f)
    def _():
        lane = _iota((tm, LANES), 1)
        pick = lane == e
        ge = jnp.sum(jnp.where(pick, gates_ref[...], 0.0), axis=-1, keepdims=True)
        pos_col = jnp.sum(jnp.where(pick, pos_ref[...], 0.0), axis=-1, keepdims=True)

        def scatter(c, carry):
            r0 = pl.multiple_of(c * rows, rows)
            slot = (r0 + _iota((1, rows), 1)).astype(F32)
            sel_t = jnp.where(pos_col == slot, 1.0, 0.0).astype(BF16)
            acc_ref[...] += ge * _dot(sel_t, y_ref[pl.ds(r0, rows), :].astype(BF16))
            return carry

        lax.fori_loop(0, n_chunks, scatter, 0)

    @pl.when((e == n_experts - 1) & (f == last_f))
    def _():
        x2 = x_ref[...] + gate_ref[...] * acc_ref[...].reshape(tb, ts, d)
        if final_norm:
            ms = jnp.mean(x2 * x2, axis=-1, keepdims=True)
            x2 = x2 * lax.rsqrt(ms + EPS) * gf_ref[...]
        o_ref[...] = x2


def _moe_ffn(x, h, logits, w_up, w_down, gate, g_final, final_norm):
    b, t, d = x.shape
    n_experts, dff = w_down.shape[:2]
    tf = _ffn_tile(dff, cap=1024)
    nf = dff // tf
    tb, ts = _row_tiling(b, t, ROW_TILE)
    tm, nt = tb * ts, t // ts
    x_spec = pl.BlockSpec((tb, ts, d), lambda i, j, e, f: (i, j, 0))
    rows = min(MOE_CHUNK_ROWS, tm)
    buf_rows = -(-tm // rows) * rows
    return pl.pallas_call(
        functools.partial(_moe_kernel, n_experts=n_experts, rows=rows, final_norm=final_norm),
        grid=(b // tb, nt, n_experts, nf),
        in_specs=[pl.BlockSpec((tm, d), lambda i, j, e, f: (i * nt + j, 0)),
                  pl.BlockSpec((tm, LANES), lambda i, j, e, f: (i * nt + j, 0)),
                  pl.BlockSpec((1, d, tf), lambda i, j, e, f: (e, 0, f)),
                  pl.BlockSpec((1, d, tf), lambda i, j, e, f: (e, 0, nf + f)),
                  pl.BlockSpec((1, tf, d), lambda i, j, e, f: (e, f, 0)),
                  x_spec,
                  pl.BlockSpec((tb, 1, d), lambda i, j, e, f: (i, 0, 0)),
                  pl.BlockSpec((1, 1, d), lambda i, j, e, f: (0, 0, 0))],
        out_specs=x_spec,
        out_shape=jax.ShapeDtypeStruct((b, t, d), F32),
        scratch_shapes=[pltpu.VMEM((tm, d), F32), pltpu.VMEM((tm, LANES), F32),
                        pltpu.VMEM((tm, LANES), F32), pltpu.VMEM((LANES, tm), F32),
                        pltpu.VMEM((buf_rows, d), BF16), pltpu.VMEM((buf_rows, d), F32),
                        pltpu.SMEM((n_experts,), jnp.int32)],
        compiler_params=_params(("parallel", "parallel", "arbitrary", "arbitrary")),
        name="moe_swiglu_ffn",
    )(h, logits, w_up, w_up, w_down, x, gate, g_final)


def _final_norm_kernel(x_ref, g_ref, o_ref):
    x = x_ref[...]
    ms = jnp.mean(x * x, axis=-1, keepdims=True)
    o_ref[...] = x * lax.rsqrt(ms + EPS) * g_ref[...]


def _final_norm(x, g):
    b, t, d = x.shape
    tb, ts = _row_tiling(b, t, ROW_TILE)
    x_spec = pl.BlockSpec((tb, ts, d), lambda i, j: (i, j, 0))
    return pl.pallas_call(
        _final_norm_kernel,
        grid=(b // tb, t // ts),
        in_specs=[x_spec, pl.BlockSpec(g.shape, lambda i, j: (0, 0, 0))],
        out_specs=x_spec,
        out_shape=jax.ShapeDtypeStruct((b, t, d), F32),
        compiler_params=_params(("parallel", "parallel")),
        name="final_rmsnorm",
    )(x, g)


def _arrange_w_in(w, widths):
    pieces, a = [], 0
    for wd in widths:
        pieces.append(w[:, a:a + wd])
        a += wd
    rank = pieces.pop(4)
    pieces.append(jnp.pad(rank, ((0, 0), (0, LANES - rank.shape[1]))))
    return jnp.concatenate(pieces, axis=1).astype(BF16)


def kernel(x_prompt, x_sample, c_prompt, c_sample, cache_sb_k, cache_sb_v, state_gla, page_table, w_ada, b_ada, norm_attn, norm_ffn, w_in, w_gate_up, b_gate, norm_gla, sb_bias, norm_sb, w_out, w_ff_up, w_ff_down, w_router, w_exp_up, w_exp_down, norm_final):
    depth, d = norm_attn.shape
    _, db, nh_gla, dk, dv = state_gla.shape
    _, n_phys, page, nh_sb, hd = cache_sb_k.shape
    rank, qk = w_gate_up.shape[1:]
    gw, sw = nh_gla * dv, nh_sb * hd
    widths = (qk, qk, gw, gw, rank, sw, sw, sw)
    bp, tp, _ = x_prompt.shape
    _, tsm, _ = x_sample.shape
    n_experts = w_router.shape[-1]

    mod = _modulation(jnp.concatenate([c_prompt, c_sample], axis=0), w_ada, b_ada)
    cache_k = cache_sb_k.transpose(0, 1, 3, 4, 2).reshape(depth, n_phys, sw, page)
    cache_v = cache_sb_v.transpose(0, 1, 3, 4, 2).reshape(depth, n_phys, sw, page)
    gla_zero = jnp.zeros((bp, nh_gla, dk, dv), F32)
    g_final = norm_final.reshape(1, 1, d)

    groups = {
        "prompt": dict(x=x_prompt, rows=slice(0, bp), s0=lambda l: gla_zero),
        "sample": dict(x=x_sample, rows=slice(bp, bp + db), s0=lambda l: state_gla[l]),
    }
    collected = {name: dict(k=[], v=[], s=[]) for name in groups}

    for l in range(depth):
        w_in_l = _arrange_w_in(w_in[l], widths)
        wg_l = jnp.pad(w_gate_up[l], ((0, LANES - rank), (0, 0)))
        bg_l = b_gate[l].reshape(1, qk)
        w_out_l = w_out[l].astype(BF16)
        moe = l % 2 == 1
        if moe:
            w_r = jnp.pad(w_router[l // 2], ((0, 0), (0, LANES - n_experts)))
            w_up_l = w_exp_up[l // 2].astype(BF16)
            w_down_l = w_exp_down[l // 2].astype(BF16)
        else:
            w_up_l = w_ff_up[l // 2].astype(BF16)
            w_down_l = w_ff_down[l // 2].astype(BF16)
        for name, grp in groups.items():
            x = grp["x"]
            b, t, _ = x.shape
            m = [mod[l, grp["rows"], i * d:(i + 1) * d].reshape(b, 1, d) for i in range(6)]
            shift1, scale1, gate1, shift2, scale2, gate2 = m
            gq, gk, gv, gg, la, sq, sk, sv, skb, svb = _input_projection(
                x, shift1, scale1, norm_attn[l].reshape(1, 1, d), w_in_l, wg_l, bg_l,
                qk=qk, gw=gw, sw=sw, dk=dk, hd=hd)
            o_gla, s_new = _gla(gq.reshape(b, t, qk), gk.reshape(b, t, qk), gv.reshape(b, t, gw),
                                la.reshape(b, t, qk), gg.reshape(b, t, gw), grp["s0"](l),
                                norm_gla[l].reshape(1, gw))
            g_sb = norm_sb[l].reshape(1, sw)
            if name == "prompt":
                o_sb = _sb_prompt(sq.reshape(b, t, sw), skb.reshape(b, t, sw), svb.reshape(b, t, sw),
                                  sb_bias[l], g_sb, hd=hd)
            else:
                o_sb = _sb_sample(sq.reshape(b, t, sw), skb.reshape(b, t, sw), svb.reshape(b, t, sw),
                                  cache_k, cache_v, l, page_table, sb_bias[l], g_sb, nh=nh_sb, hd=hd)
            res = _output_projection(x, o_gla.reshape(b * t, gw), o_sb.reshape(b * t, sw), w_out_l,
                                     gate1, shift2, scale2, norm_ffn[l].reshape(1, 1, d),
                                     w_r if moe else None)
            if moe:
                x1, h2, logits = res
                x = _moe_ffn(x1, h2, logits, w_up_l, w_down_l, gate2, g_final, l == depth - 1)
            else:
                x1, h2 = res
                x = _dense_ffn(x1, h2, w_up_l, w_down_l, gate2)
            grp["x"] = x
            collected[name]["k"].append(sk.reshape(b, t, nh_sb, hd))
            collected[name]["v"].append(sv.reshape(b, t, nh_sb, hd))
            collected[name]["s"].append(s_new)

    y_prompt, y_sample = groups["prompt"]["x"], groups["sample"]["x"]
    if (depth - 1) % 2 == 0:
        y_prompt, y_sample = _final_norm(y_prompt, g_final), _final_norm(y_sample, g_final)
    cp, cs = collected["prompt"], collected["sample"]
    return (y_prompt, y_sample, jnp.stack(cp["k"]), jnp.stack(cp["v"]), jnp.stack(cp["s"]),
            jnp.stack(cs["k"]), jnp.stack(cs["v"]), jnp.stack(cs["s"]))
```

```python
import functools

import jax
import jax.numpy as jnp
from jax import lax
from jax.experimental import pallas as pl
from jax.experimental.pallas import tpu as pltpu

F32 = jnp.float32
BF16 = jnp.bfloat16
EPS = 1e-6
GATE_TAU = 16.0
GLA_CHUNK = 64
GLA_SUB = 16
TOP_K = 2
LANES = 128
SB_BLOCK = 256
SB_QUERY_BLOCKS = 4
SB_SWEEP_BLOCKS = 4
PAGES_PER_STEP = 16
LOG2E = 1.4426950408889634
SOFTPLUS2_CLAMP = 126.0
ROW_TILE = 1024
MOE_CHUNK_ROWS = 128
VMEM_LIMIT = 56 * 1024 * 1024


def _params(sem):
    return pltpu.CompilerParams(dimension_semantics=sem, vmem_limit_bytes=VMEM_LIMIT)


def _dot(a, b):
    return jnp.dot(a, b, preferred_element_type=F32)


def _dot_nt(a, b):
    return lax.dot_general(a, b, (((1,), (1,)), ((), ())), preferred_element_type=F32)


def _dot_tn(a, b):
    return lax.dot_general(a, b, (((0,), (0,)), ((), ())), preferred_element_type=F32)


def _split(x):
    hi = x.astype(BF16)
    return hi, (x - hi.astype(F32)).astype(BF16)


def _dot_hp(a, b):
    ah, al = _split(a)
    bh, bl = _split(b)
    return _dot(ah, bh) + (_dot(ah, bl) + _dot(al, bh))


def _dot_hp_exact_rhs(a, b_bf16):
    ah, al = _split(a)
    return _dot(ah, b_bf16) + _dot(al, b_bf16)


def _silu(x):
    return x * (1.0 / (1.0 + jnp.exp(-x)))


def _log_sigmoid(x):
    return jnp.minimum(x, 0.0) - jnp.log1p(jnp.exp(-jnp.abs(x)))


def _idiv(x, n):
    if n & (n - 1) == 0:
        return lax.shift_right_logical(x, n.bit_length() - 1)
    return x // n


def _iota(shape, dim):
    return lax.broadcasted_iota(jnp.int32, shape, dim)


def _modulated_norm(x, g, scale, shift):
    ms = jnp.mean(x * x, axis=-1, keepdims=True)
    return (x * lax.rsqrt(ms + EPS) * g) * (1.0 + scale) + shift


def _row_tiling(b, t, target):
    if t >= target:
        assert t % target == 0
        return 1, target
    tb = min(b, max(1, target // t))
    assert b % tb == 0
    return tb, t


def _mod_kernel(c_ref, w_ref, b_ref, o_ref):
    o_ref[0] = _dot_hp(_silu(c_ref[...]), w_ref[0]) + b_ref[0]


def _modulation(c, w_ada, b_ada):
    depth, d, n = w_ada.shape
    bc = c.shape[0]
    tn = n // 4
    return pl.pallas_call(
        _mod_kernel,
        grid=(depth, n // tn),
        in_specs=[pl.BlockSpec((bc, d), lambda l, j: (0, 0)),
                  pl.BlockSpec((1, d, tn), lambda l, j: (l, 0, j)),
                  pl.BlockSpec((1, 1, tn), lambda l, j: (l, 0, j))],
        out_specs=pl.BlockSpec((1, bc, tn), lambda l, j: (l, 0, j)),
        out_shape=jax.ShapeDtypeStruct((depth, bc, n), F32),
        compiler_params=_params(("parallel", "parallel")),
        name="adaln_modulation",
    )(c, w_ada, b_ada.reshape(depth, 1, n))


def _inproj_kernel(x_ref, shift_ref, scale_ref, g_ref, w_ref, wg_ref, bg_ref,
                   gq_ref, gk_ref, gv_ref, gg_ref, la_ref, sq_ref, sk_ref, sv_ref, skb_ref, svb_ref,
                   *, qk, gw, sw, q_scale_gla, q_scale_sb):
    tb, ts, d = x_ref.shape
    h = _modulated_norm(x_ref[...], g_ref[...], scale_ref[...], shift_ref[...])
    hb = h.reshape(tb * ts, d).astype(BF16)
    off = [0]

    def proj(width):
        a = off[0]
        off[0] = a + width
        return _dot(hb, w_ref[:, a:a + width])

    gq_ref[...] = proj(qk) * q_scale_gla
    gk_ref[...] = proj(qk)
    gv_ref[...] = proj(gw)
    gg_ref[...] = proj(gw)
    sq_ref[...] = (proj(sw) * q_scale_sb).astype(BF16)
    sk = proj(sw)
    sk_ref[...] = sk
    skb_ref[...] = sk.astype(BF16)
    sv = proj(sw)
    sv_ref[...] = sv
    svb_ref[...] = sv.astype(BF16)
    ga = proj(LANES)
    gate = _dot_hp(ga, wg_ref[...]) + bg_ref[...]
    la_ref[...] = _log_sigmoid(gate) * (1.0 / GATE_TAU)


def _input_projection(x, shift, scale, g, w, wg, bg, *, qk, gw, sw, dk, hd):
    b, t, d = x.shape
    n = b * t
    tb, ts = _row_tiling(b, t, ROW_TILE // 2)
    tm = tb * ts
    nt = t // ts
    x_spec = pl.BlockSpec((tb, ts, d), lambda i, j: (i, j, 0))
    m_spec = pl.BlockSpec((tb, 1, d), lambda i, j: (i, 0, 0))

    def full(a):
        return pl.BlockSpec(a.shape, lambda i, j: (0,) * a.ndim)

    def out(width, dtype):
        return (pl.BlockSpec((tm, width), lambda i, j: (i * nt + j, 0)),
                jax.ShapeDtypeStruct((n, width), dtype))

    outs = [out(qk, F32), out(qk, F32), out(gw, F32), out(gw, F32), out(qk, F32),
            out(sw, BF16), out(sw, F32), out(sw, F32), out(sw, BF16), out(sw, BF16)]
    kern = functools.partial(_inproj_kernel, qk=qk, gw=gw, sw=sw,
                             q_scale_gla=dk ** -0.5, q_scale_sb=hd ** -0.5 * LOG2E)
    return pl.pallas_call(
        kern,
        grid=(b // tb, nt),
        in_specs=[x_spec, m_spec, m_spec, full(g), full(w), full(wg), full(bg)],
        out_specs=[o[0] for o in outs],
        out_shape=[o[1] for o in outs],
        compiler_params=_params(("parallel", "parallel")),
        name="norm_input_projection",
    )(x, shift, scale, g, w, wg, bg)


def _gla_kernel(q_ref, k_ref, v_ref, la_ref, gg_ref, s0_ref, g_ref, o_ref, s_out_ref, st_ref,
                *, chunk, sub):
    _, nh, dk, dv = s0_ref.shape
    tblock = q_ref.shape[1]
    qk, vw = nh * dk, nh * dv
    t = pl.program_id(1)

    @pl.when(t == 0)
    def _():
        rows = []
        for h in range(nh):
            rows.append(jnp.concatenate(
                [s0_ref[0, h] if hh == h else jnp.zeros((dk, dv), F32) for hh in range(nh)], axis=1))
        st_ref[...] = jnp.concatenate(rows, axis=0).T

    state_mask = _idiv(_iota((vw, qk), 0), dv) == _idiv(_iota((vw, qk), 1), dk)
    head_expand = (_idiv(_iota((qk, vw), 0), dk) == _idiv(_iota((qk, vw), 1), dv)).astype(BF16)
    key_mask = _idiv(_iota((nh * sub, qk), 0), sub) == _idiv(_iota((nh * sub, qk), 1), dk)
    val_mask = _idiv(_iota((nh * sub, vw), 0), sub) == _idiv(_iota((nh * sub, vw), 1), dv)
    tril = (_iota((chunk, chunk), 0) >= _iota((chunk, chunk), 1)).astype(BF16)
    row_id = _iota((chunk, 1), 0)
    pair_mask = _iota((sub, sub, 1), 0) >= _iota((sub, sub, 1), 1)
    n_sub = chunk // sub

    def one_chunk(ci, carry):
        r = pl.multiple_of(ci * chunk, chunk)
        q = q_ref[0, pl.ds(r, chunk), :]
        k = k_ref[0, pl.ds(r, chunk), :]
        v = v_ref[0, pl.ds(r, chunk), :]
        la_hi, la_lo = _split(la_ref[0, pl.ds(r, chunk), :])
        cum = _dot(tril, la_hi) + _dot(tril, la_lo)
        last = cum[chunk - 1:chunk, :]
        st = st_ref[...]

        o = _dot_nt((q * jnp.exp(cum)).astype(BF16), st.astype(BF16))
        k_dec = (k * jnp.exp(last - cum)).astype(BF16)
        upd = _dot_tn(v.astype(BF16), k_dec)
        st_ref[...] = st * jnp.exp(last) + jnp.where(state_mask, upd, 0.0)

        for j in range(n_sub - 1):
            a = j * sub
            ref_row = cum[a + sub - 1:a + sub, :]
            q_dec = (q * jnp.exp(jnp.minimum(cum - ref_row, 0.0))).astype(BF16)
            k_j = k[a:a + sub] * jnp.exp(ref_row - cum[a:a + sub])
            k_bd = jnp.where(key_mask, jnp.concatenate([k_j] * nh, axis=0), 0.0).astype(BF16)
            sc = _dot_nt(q_dec, k_bd)
            sc = jnp.where(row_id >= a + sub, sc, 0.0).astype(BF16)
            v_bd = jnp.where(val_mask, jnp.concatenate([v[a:a + sub]] * nh, axis=0), 0.0).astype(BF16)
            o = o + _dot(sc, v_bd)

        diag = []
        for j in range(n_sub):
            a = j * sub
            l_j, q_j, k_j, v_j = cum[a:a + sub], q[a:a + sub], k[a:a + sub], v[a:a + sub]
            dec = jnp.exp(jnp.minimum(l_j[:, None, :] - l_j[None, :, :], 0.0))
            p = jnp.where(pair_mask, (q_j[:, None, :] * k_j[None, :, :]) * dec, 0.0)
            sc = _dot(p.reshape(sub * sub, qk).astype(BF16), head_expand)
            diag.append(jnp.sum(sc.reshape(sub, sub, vw) * v_j[None, :, :], axis=1))
        o = o + (diag[0] if n_sub == 1 else jnp.concatenate(diag, axis=0))

        g = g_ref[...]
        gg = gg_ref[0, pl.ds(r, chunk), :]
        outs = []
        for h in range(nh):
            oh = o[:, h * dv:(h + 1) * dv]
            ms = jnp.mean(oh * oh, axis=-1, keepdims=True)
            outs.append(oh * lax.rsqrt(ms + EPS) * g[:, h * dv:(h + 1) * dv])
        o_ref[0, pl.ds(r, chunk), :] = jnp.concatenate(outs, axis=1) * _silu(gg)
        return carry

    lax.fori_loop(0, tblock // chunk, one_chunk, 0)

    @pl.when(t == pl.num_programs(1) - 1)
    def _():
        s_bd = st_ref[...].T
        for h in range(nh):
            s_out_ref[0, h] = s_bd[h * dk:(h + 1) * dk, h * dv:(h + 1) * dv]


def _gla(q, k, v, la, gg, s0, g):
    b, t, qk = q.shape
    vw = v.shape[-1]
    chunk = GLA_CHUNK if t % GLA_CHUNK == 0 else t
    sub = GLA_SUB if chunk % GLA_SUB == 0 else chunk
    tblock = min(t, 8 * chunk)
    assert t % tblock == 0

    def tok(width):
        return pl.BlockSpec((1, tblock, width), lambda i, j: (i, j, 0))

    s_spec = pl.BlockSpec((1,) + s0.shape[1:], lambda i, j: (i, 0, 0, 0))
    return pl.pallas_call(
        functools.partial(_gla_kernel, chunk=chunk, sub=sub),
        grid=(b, t // tblock),
        in_specs=[tok(qk), tok(qk), tok(vw), tok(qk), tok(vw), s_spec,
                  pl.BlockSpec(g.shape, lambda i, j: (0, 0))],
        out_specs=[tok(vw), s_spec],
        out_shape=[jax.ShapeDtypeStruct((b, t, vw), F32), jax.ShapeDtypeStruct(s0.shape, F32)],
        scratch_shapes=[pltpu.VMEM((vw, qk), F32)],
        compiler_params=_params(("parallel", "arbitrary")),
        name="gla_chunked",
    )(q, k, v, la, gg, s0, g)


def _softplus2(z2):
    return jnp.maximum(z2, jnp.log2(1.0 + jnp.exp2(jnp.minimum(z2, SOFTPLUS2_CLAMP))))


def _sb_within(z2, tri, valid):
    sp = _softplus2(z2)
    if valid is not None:
        sp = jnp.where(valid, sp, 0.0)
    return _dot(sp.astype(BF16), tri)


def _sb_weights(z2, within, run, valid):
    a = jnp.exp2(z2 - (within + run))
    if valid is not None:
        a = jnp.where(valid, a, 0.0)
    return a.astype(BF16), run + within[:, 0:1]


def _sb_prompt_kernel(bias_ref, q_ref, k_ref, v_ref, g_ref, o_ref, acc_ref, run_ref, *, hd, kblk):
    qblk = q_ref.shape[0]
    n_sub = qblk // kblk
    p = pl.program_id(1)
    i = pl.program_id(2)
    q = q_ref[...]
    first = _iota((1, LANES), 1) < hd
    zero = jnp.zeros_like(q)
    q_heads = (jnp.where(first, q, zero), jnp.where(first, zero, q))
    biases = (bias_ref[2 * p] * LOG2E, bias_ref[2 * p + 1] * LOG2E)
    tri = (_iota((kblk, kblk), 0) >= _iota((kblk, kblk), 1)).astype(BF16)
    acc_ref[...] = jnp.zeros_like(acc_ref)
    run_ref[...] = jnp.zeros_like(run_ref)

    def sweep(blocks, valids):
        ks = [k_ref[pl.ds(pl.multiple_of(jb * kblk, kblk), kblk), :] for jb in blocks]
        v_all = v_ref[pl.ds(pl.multiple_of(blocks[-1] * kblk, kblk), len(blocks) * kblk), :]
        for h in range(2):
            zs = [_dot_nt(q_heads[h], kb) + biases[h] for kb in ks]
            ws = [_sb_within(z, tri, valid) for z, valid in zip(zs, valids)]
            run = run_ref[h]
            parts = []
            for z, w, valid in zip(zs, ws, valids):
                a, run = _sb_weights(z, w, run, valid)
                parts.append(a)
            run_ref[h] = run
            acc_ref[h] += _dot(jnp.concatenate(parts[::-1], axis=1), v_all)

    row = _iota((qblk, kblk), 0)
    col = _iota((qblk, kblk), 1)
    diag = list(reversed(range(n_sub)))
    sweep([i * n_sub + s for s in diag], [col + s * kblk < row for s in diag])

    groups = SB_SWEEP_BLOCKS // n_sub if SB_SWEEP_BLOCKS % n_sub == 0 else 1
    per_iter = groups * n_sub
    odd = lax.rem(i, groups)
    for r in range(1, groups):
        @pl.when(odd >= r)
        def _():
            base = (i - r) * n_sub
            sweep([base + s for s in diag], [None] * n_sub)
    top = (i - odd) * n_sub
    desc = list(reversed(range(per_iter)))

    def body(j, carry):
        base = top - (j + 1) * per_iter
        sweep([base + s for s in desc], [None] * per_iter)
        return carry

    lax.fori_loop(0, i // groups, body, 0)

    o = jnp.where(first, acc_ref[0], acc_ref[1])
    ss = o * o
    s0 = jnp.sum(jnp.where(first, ss, 0.0), axis=-1, keepdims=True)
    s1 = jnp.sum(jnp.where(first, 0.0, ss), axis=-1, keepdims=True)
    ms = jnp.where(first, s0, s1) * (1.0 / hd)
    o_ref[...] = (o * lax.rsqrt(ms + EPS) * g_ref[...]).astype(BF16)


def _sb_prompt(q, k, v, bias, g, *, hd):
    b, t, w = q.shape
    kblk = SB_BLOCK if t % SB_BLOCK == 0 else t
    blk = SB_QUERY_BLOCKS * kblk if t % (SB_QUERY_BLOCKS * kblk) == 0 else kblk
    n_pairs = w // LANES
    return pl.pallas_call(
        functools.partial(_sb_prompt_kernel, hd=hd, kblk=kblk),
        grid=(b, n_pairs, t // blk),
        in_specs=[pl.BlockSpec(memory_space=pltpu.SMEM),
                  pl.BlockSpec((None, blk, LANES), lambda bi, p, i: (bi, i, p)),
                  pl.BlockSpec((None, t, LANES), lambda bi, p, i: (bi, 0, p)),
                  pl.BlockSpec((None, t, LANES), lambda bi, p, i: (bi, 0, p)),
                  pl.BlockSpec((1, LANES), lambda bi, p, i: (0, p))],
        out_specs=pl.BlockSpec((None, blk, LANES), lambda bi, p, i: (bi, i, p)),
        out_shape=jax.ShapeDtypeStruct((b, t, w), BF16),
        scratch_shapes=[pltpu.VMEM((2, blk, LANES), F32), pltpu.VMEM((2, blk, 1), F32)],
        compiler_params=_params(("parallel", "parallel", "arbitrary")),
        name="sb_attention_prompt",
    )(bias, q, k, v, g)


def _sb_sample_kernel(pt_ref, bias_ref, q_ref, kn_ref, vn_ref, g_ref, *rest, nh, hd, pages):
    k_pages, v_pages = rest[:pages], rest[pages:2 * pages]
    o_ref, acc_ref, run_ref = rest[2 * pages:]
    del pt_ref
    ts, w = q_ref.shape[1:]
    page = k_pages[0].shape[1]
    rows = nh * ts
    j = pl.program_id(1)

    row_head = _idiv(_iota((rows, w), 0), ts)
    col_head = _idiv(_iota((rows, w), 1), hd)
    q = jnp.concatenate([q_ref[0].astype(F32)] * nh, axis=0)
    q_bd = jnp.where(row_head == col_head, q, 0.0).astype(BF16)
    rh = _idiv(_iota((rows, 1), 0), ts)
    bias = jnp.zeros((rows, 1), F32)
    for h in range(nh):
        bias = jnp.where(rh == h, bias_ref[h] * LOG2E, bias)

    def tri(n):
        return (_iota((n, n), 0) >= _iota((n, n), 1)).astype(BF16)

    @pl.when(j == 0)
    def _():
        pad = jnp.zeros((LANES - ts, w), F32)
        kn = jnp.concatenate([kn_ref[0].astype(F32), pad], axis=0).astype(BF16)
        vn = jnp.concatenate([vn_ref[0].astype(F32), pad], axis=0).astype(BF16)
        z = _dot_nt(q_bd, kn) + bias
        tok = _iota((rows, LANES), 0) - _idiv(_iota((rows, LANES), 0), ts) * ts
        valid = _iota((rows, LANES), 1) < tok
        a, run = _sb_weights(z, _sb_within(z, tri(LANES), valid), jnp.zeros((rows, 1), F32), valid)
        run_ref[...] = run
        acc_ref[...] = _dot(a, vn)

    blk = 2 * page
    tri2 = tri(blk)
    kt = jnp.concatenate([r[...] for r in k_pages], axis=1).astype(BF16)
    vt = jnp.concatenate([r[...] for r in v_pages], axis=1).astype(BF16)
    z = _dot(q_bd, kt) + bias
    n_blk = pages // 2
    zs = [z[:, m * blk:(m + 1) * blk] for m in range(n_blk)]
    ws = [_sb_within(zm, tri2, None) for zm in zs]
    run = run_ref[...]
    parts = [None] * n_blk
    for m in reversed(range(n_blk)):
        parts[m], run = _sb_weights(zs[m], ws[m], run, None)
    run_ref[...] = run
    acc_ref[...] += _dot_nt(jnp.concatenate(parts, axis=1), vt)

    @pl.when(j == pl.num_programs(1) - 1)
    def _():
        acc = jnp.where(row_head == col_head, acc_ref[...], 0.0)
        o = jnp.sum(acc.reshape(nh, ts, w), axis=0)
        same_head = (_idiv(_iota((w, w), 0), hd) == _idiv(_iota((w, w), 1), hd)).astype(BF16)
        ms = _dot_hp_exact_rhs(o * o, same_head) * (1.0 / hd)
        o_ref[0] = (o * lax.rsqrt(ms + EPS) * g_ref[...]).astype(BF16)


def _sb_sample(q, k_new, v_new, cache_k, cache_v, layer, page_table, bias, g, *, nh, hd):
    b, ts, w = q.shape
    n_pages = page_table.shape[1]
    page = cache_k.shape[3]
    pages = PAGES_PER_STEP if n_pages % PAGES_PER_STEP == 0 else 2
    assert n_pages % pages == 0 and pages % 2 == 0
    steps = n_pages // pages

    def tok_spec():
        return pl.BlockSpec((1, ts, w), lambda i, j, pt: (i, 0, 0))

    def page_spec(r):
        return pl.BlockSpec((None, None, w, page),
                            lambda i, j, pt: (layer, pt[i, (steps - 1 - j) * pages + r], 0, 0))

    grid_spec = pltpu.PrefetchScalarGridSpec(
        num_scalar_prefetch=1,
        grid=(b, steps),
        in_specs=[pl.BlockSpec(memory_space=pltpu.SMEM), tok_spec(), tok_spec(), tok_spec(),
                  pl.BlockSpec(g.shape, lambda i, j, pt: (0, 0))]
                 + [page_spec(r) for r in range(pages)] * 2,
        out_specs=tok_spec(),
        scratch_shapes=[pltpu.VMEM((nh * ts, w), F32), pltpu.VMEM((nh * ts, 1), F32)],
    )
    return pl.pallas_call(
        functools.partial(_sb_sample_kernel, nh=nh, hd=hd, pages=pages),
        grid_spec=grid_spec,
        out_shape=jax.ShapeDtypeStruct((b, ts, w), BF16),
        compiler_params=_params(("parallel", "arbitrary")),
        name="sb_attention_paged",
    )(page_table, bias, q, k_new, v_new, g, *([cache_k] * pages), *([cache_v] * pages))


def _outproj_kernel(x_ref, og_ref, os_ref, w_ref, gate_ref, shift_ref, scale_ref, g_ref, *rest):
    tb, ts, d = x_ref.shape
    gw = og_ref.shape[1]
    mixed = _dot(og_ref[...].astype(BF16), w_ref[0:gw, :]) + _dot(os_ref[...], w_ref[gw:, :])
    x1 = x_ref[...] + gate_ref[...] * mixed.reshape(tb, ts, d)
    h = _modulated_norm(x1, g_ref[...], scale_ref[...], shift_ref[...]).reshape(tb * ts, d)
    if len(rest) == 4:
        wr_ref, x1_ref, h_ref, lg_ref = rest
        lg_ref[...] = _dot_hp(h, wr_ref[...])
    else:
        x1_ref, h_ref = rest
    x1_ref[...] = x1
    h_ref[...] = h.astype(BF16)


def _output_projection(x, o_gla, o_sb, w_out, gate, shift, scale, g, w_router=None):
    b, t, d = x.shape
    n = b * t
    tb, ts = _row_tiling(b, t, ROW_TILE // 2)
    tm, nt = tb * ts, t // ts
    x_spec = pl.BlockSpec((tb, ts, d), lambda i, j: (i, j, 0))
    m_spec = pl.BlockSpec((tb, 1, d), lambda i, j: (i, 0, 0))

    def rows(width):
        return pl.BlockSpec((tm, width), lambda i, j: (i * nt + j, 0))

    def full(a):
        return pl.BlockSpec(a.shape, lambda i, j: (0,) * a.ndim)

    in_specs = [x_spec, rows(o_gla.shape[1]), rows(o_sb.shape[1]), full(w_out), m_spec, m_spec, m_spec, full(g)]
    args = [x, o_gla, o_sb, w_out, gate, shift, scale, g]
    out_specs = [x_spec, rows(d)]
    out_shape = [jax.ShapeDtypeStruct((b, t, d), F32), jax.ShapeDtypeStruct((n, d), BF16)]
    if w_router is not None:
        in_specs.append(full(w_router))
        args.append(w_router)
        out_specs.append(rows(LANES))
        out_shape.append(jax.ShapeDtypeStruct((n, LANES), F32))
    return pl.pallas_call(
        _outproj_kernel,
        grid=(b // tb, nt),
        in_specs=in_specs, out_specs=out_specs, out_shape=out_shape,
        compiler_params=_params(("parallel", "parallel")),
        name="output_projection_norm",
    )(*args)


def _ffn_kernel(h_ref, wg_ref, wv_ref, wd_ref, x_ref, gate_ref, o_ref, acc_ref):
    tb, ts, d = x_ref.shape
    f = pl.program_id(2)

    @pl.when(f == 0)
    def _():
        acc_ref[...] = jnp.zeros_like(acc_ref)

    h = h_ref[...]
    act = _silu(_dot(h, wg_ref[...])) * _dot(h, wv_ref[...])
    acc_ref[...] += _dot(act.astype(BF16), wd_ref[...])

    @pl.when(f == pl.num_programs(2) - 1)
    def _():
        o_ref[...] = x_ref[...] + gate_ref[...] * acc_ref[...].reshape(tb, ts, d)


def _ffn_tile(dff, cap=512):
    for tf in range(cap - cap % LANES, 0, -LANES):
        if dff % tf == 0:
            return tf
    raise ValueError(dff)


def _dense_ffn(x, h, w_up, w_down, gate):
    b, t, d = x.shape
    dff = w_down.shape[0]
    tf = _ffn_tile(dff)
    nf = dff // tf
    tb, ts = _row_tiling(b, t, ROW_TILE)
    tm, nt = tb * ts, t // ts
    x_spec = pl.BlockSpec((tb, ts, d), lambda i, j, f: (i, j, 0))
    return pl.pallas_call(
        _ffn_kernel,
        grid=(b // tb, nt, nf),
        in_specs=[pl.BlockSpec((tm, d), lambda i, j, f: (i * nt + j, 0)),
                  pl.BlockSpec((d, tf), lambda i, j, f: (0, f)),
                  pl.BlockSpec((d, tf), lambda i, j, f: (0, nf + f)),
                  pl.BlockSpec((tf, d), lambda i, j, f: (f, 0)),
                  x_spec,
                  pl.BlockSpec((tb, 1, d), lambda i, j, f: (i, 0, 0))],
        out_specs=x_spec,
        out_shape=jax.ShapeDtypeStruct((b, t, d), F32),
        scratch_shapes=[pltpu.VMEM((tm, d), F32)],
        compiler_params=_params(("parallel", "parallel", "arbitrary")),
        name="swiglu_ffn",
    )(h, w_up, w_up, w_down, x, gate)


def _top2_gates(logits, n_experts):
    lane = _iota(logits.shape, 1).astype(F32)
    neg = jnp.float32(-jnp.inf)
    l1 = jnp.where(lane < n_experts, logits, neg)
    m1 = jnp.max(l1, axis=-1, keepdims=True)
    i1 = jnp.min(jnp.where(l1 == m1, lane, float(LANES)), axis=-1, keepdims=True)
    l2 = jnp.where(lane == i1, neg, l1)
    m2 = jnp.max(l2, axis=-1, keepdims=True)
    i2 = jnp.min(jnp.where(l2 == m2, lane, float(LANES)), axis=-1, keepdims=True)
    p2 = jnp.exp(m2 - m1)
    w1 = 1.0 / (1.0 + p2)
    return jnp.where(lane == i1, w1, 0.0) + jnp.where(lane == i2, p2 * w1, 0.0)


def _moe_kernel(h_ref, lg_ref, wg_ref, wv_ref, wd_ref, x_ref, gate_ref, gf_ref, o_ref,
                acc_ref, gates_ref, pos_ref, pos_t_ref, xg_ref, y_ref, count_ref,
                *, n_experts, rows, final_norm):
    tb, ts, d = x_ref.shape
    tm = tb * ts
    e = pl.program_id(2)
    f = pl.program_id(3)
    last_f = pl.num_programs(3) - 1

    @pl.when((e == 0) & (f == 0))
    def _():
        acc_ref[...] = jnp.zeros_like(acc_ref)
        gates = _top2_gates(lg_ref[...], n_experts)
        gates_ref[...] = gates
        routed = gates != 0.0
        before = (_iota((tm, tm), 0) > _iota((tm, tm), 1)).astype(BF16)
        pos = jnp.where(routed, _dot(before, jnp.where(routed, 1.0, 0.0).astype(BF16)), -1.0)
        pos_ref[...] = pos
        pos_t = pos.T
        pos_t_ref[...] = pos_t
        for ee in range(n_experts):
            count_ref[ee] = (jnp.max(pos_t[ee:ee + 1, :]) + 1.0).astype(jnp.int32)

    pos_row = pos_t_ref[pl.ds(e, 1), :]
    n_chunks = _idiv(count_ref[e] + (rows - 1), rows)

    @pl.when(f == 0)
    def _():
        def gather(c, carry):
            r0 = pl.multiple_of(c * rows, rows)
            slot = (r0 + _iota((rows, 1), 0)).astype(F32)
            sel = jnp.where(pos_row == slot, 1.0, 0.0).astype(BF16)
            xg_ref[pl.ds(r0, rows), :] = _dot(sel, h_ref[...]).astype(BF16)
            y_ref[pl.ds(r0, rows), :] = jnp.zeros((rows, d), F32)
            return carry

        lax.fori_loop(0, n_chunks, gather, 0)

    def expert(c, carry):
        r0 = pl.multiple_of(c * rows, rows)
        xc = xg_ref[pl.ds(r0, rows), :]
        act = _silu(_dot(xc, wg_ref[0])) * _dot(xc, wv_ref[0])
        y_ref[pl.ds(r0, rows), :] += _dot(act.astype(BF16), wd_ref[0])
        return carry

    lax.fori_loop(0, n_chunks, expert, 0)

    @pl.when(f == last_f)
    def _():
        lane = _iota((tm, LANES), 1)
        pick = lane == e
        ge = jnp.sum(jnp.where(pick, gates_ref[...], 0.0), axis=-1, keepdims=True)
        pos_col = jnp.sum(jnp.where(pick, pos_ref[...], 0.0), axis=-1, keepdims=True)

        def scatter(c, carry):
            r0 = pl.multiple_of(c * rows, rows)
            slot = (r0 + _iota((1, rows), 1)).astype(F32)
            sel_t = jnp.where(pos_col == slot, 1.0, 0.0).astype(BF16)
            acc_ref[...] += ge * _dot(sel_t, y_ref[pl.ds(r0, rows), :].astype(BF16))
            return carry

        lax.fori_loop(0, n_chunks, scatter, 0)

    @pl.when((e == n_experts - 1) & (f == last_f))
    def _():
        x2 = x_ref[...] + gate_ref[...] * acc_ref[...].reshape(tb, ts, d)
        if final_norm:
            ms = jnp.mean(x2 * x2, axis=-1, keepdims=True)
            x2 = x2 * lax.rsqrt(ms + EPS) * gf_ref[...]
        o_ref[...] = x2


def _moe_ffn(x, h, logits, w_up, w_down, gate, g_final, final_norm):
    b, t, d = x.shape
    n_experts, dff = w_down.shape[:2]
    tf = _ffn_tile(dff, cap=1024)
    nf = dff // tf
    tb, ts = _row_tiling(b, t, ROW_TILE)
    tm, nt = tb * ts, t // ts
    x_spec = pl.BlockSpec((tb, ts, d), lambda i, j, e, f: (i, j, 0))
    rows = min(MOE_CHUNK_ROWS, tm)
    buf_rows = -(-tm // rows) * rows
    return pl.pallas_call(
        functools.partial(_moe_kernel, n_experts=n_experts, rows=rows, final_norm=final_norm),
        grid=(b // tb, nt, n_experts, nf),
        in_specs=[pl.BlockSpec((tm, d), lambda i, j, e, f: (i * nt + j, 0)),
                  pl.BlockSpec((tm, LANES), lambda i, j, e, f: (i * nt + j, 0)),
                  pl.BlockSpec((1, d, tf), lambda i, j, e, f: (e, 0, f)),
                  pl.BlockSpec((1, d, tf), lambda i, j, e, f: (e, 0, nf + f)),
                  pl.BlockSpec((1, tf, d), lambda i, j, e, f: (e, f, 0)),
                  x_spec,
                  pl.BlockSpec((tb, 1, d), lambda i, j, e, f: (i, 0, 0)),
                  pl.BlockSpec((1, 1, d), lambda i, j, e, f: (0, 0, 0))],
        out_specs=x_spec,
        out_shape=jax.ShapeDtypeStruct((b, t, d), F32),
        scratch_shapes=[pltpu.VMEM((tm, d), F32), pltpu.VMEM((tm, LANES), F32),
                        pltpu.VMEM((tm, LANES), F32), pltpu.VMEM((LANES, tm), F32),
                        pltpu.VMEM((buf_rows, d), BF16), pltpu.VMEM((buf_rows, d), F32),
                        pltpu.SMEM((n_experts,), jnp.int32)],
        compiler_params=_params(("parallel", "parallel", "arbitrary", "arbitrary")),
        name="moe_swiglu_ffn",
    )(h, logits, w_up, w_up, w_down, x, gate, g_final)


def _final_norm_kernel(x_ref, g_ref, o_ref):
    x = x_ref[...]
    ms = jnp.mean(x * x, axis=-1, keepdims=True)
    o_ref[...] = x * lax.rsqrt(ms + EPS) * g_ref[...]


def _final_norm(x, g):
    b, t, d = x.shape
    tb, ts = _row_tiling(b, t, ROW_TILE)
    x_spec = pl.BlockSpec((tb, ts, d), lambda i, j: (i, j, 0))
    return pl.pallas_call(
        _final_norm_kernel,
        grid=(b // tb, t // ts),
        in_specs=[x_spec, pl.BlockSpec(g.shape, lambda i, j: (0, 0, 0))],
        out_specs=x_spec,
        out_shape=jax.ShapeDtypeStruct((b, t, d), F32),
        compiler_params=_params(("parallel", "parallel")),
        name="final_rmsnorm",
    )(x, g)


def _arrange_w_in(w, widths):
    pieces, a = [], 0
    for wd in widths:
        pieces.append(w[:, a:a + wd])
        a += wd
    rank = pieces.pop(4)
    pieces.append(jnp.pad(rank, ((0, 0), (0, LANES - rank.shape[1]))))
    return jnp.concatenate(pieces, axis=1).astype(BF16)


def kernel(x_prompt, x_sample, c_prompt, c_sample, cache_sb_k, cache_sb_v, state_gla, page_table, w_ada, b_ada, norm_attn, norm_ffn, w_in, w_gate_up, b_gate, norm_gla, sb_bias, norm_sb, w_out, w_ff_up, w_ff_down, w_router, w_exp_up, w_exp_down, norm_final):
    depth, d = norm_attn.shape
    _, db, nh_gla, dk, dv = state_gla.shape
    _, n_phys, page, nh_sb, hd = cache_sb_k.shape
    rank, qk = w_gate_up.shape[1:]
    gw, sw = nh_gla * dv, nh_sb * hd
    widths = (qk, qk, gw, gw, rank, sw, sw, sw)
    bp, tp, _ = x_prompt.shape
    _, tsm, _ = x_sample.shape
    n_experts = w_router.shape[-1]

    mod = _modulation(jnp.concatenate([c_prompt, c_sample], axis=0), w_ada, b_ada)
    cache_k = cache_sb_k.transpose(0, 1, 3, 4, 2).reshape(depth, n_phys, sw, page)
    cache_v = cache_sb_v.transpose(0, 1, 3, 4, 2).reshape(depth, n_phys, sw, page)
    gla_zero = jnp.zeros((bp, nh_gla, dk, dv), F32)
    g_final = norm_final.reshape(1, 1, d)

    groups = {
        "prompt": dict(x=x_prompt, rows=slice(0, bp), s0=lambda l: gla_zero),
        "sample": dict(x=x_sample, rows=slice(bp, bp + db), s0=lambda l: state_gla[l]),
    }
    collected = {name: dict(k=[], v=[], s=[]) for name in groups}

    for l in range(depth):
        w_in_l = _arrange_w_in(w_in[l], widths)
        wg_l = jnp.pad(w_gate_up[l], ((0, LANES - rank), (0, 0)))
        bg_l = b_gate[l].reshape(1, qk)
        w_out_l = w_out[l].astype(BF16)
        moe = l % 2 == 1
        if moe:
            w_r = jnp.pad(w_router[l // 2], ((0, 0), (0, LANES - n_experts)))
            w_up_l = w_exp_up[l // 2].astype(BF16)
            w_down_l = w_exp_down[l // 2].astype(BF16)
        else:
            w_up_l = w_ff_up[l // 2].astype(BF16)
            w_down_l = w_ff_down[l // 2].astype(BF16)
        for name, grp in groups.items():
            x = grp["x"]
            b, t, _ = x.shape
            m = [mod[l, grp["rows"], i * d:(i + 1) * d].reshape(b, 1, d) for i in range(6)]
            shift1, scale1, gate1, shift2, scale2, gate2 = m
            gq, gk, gv, gg, la, sq, sk, sv, skb, svb = _input_projection(
                x, shift1, scale1, norm_attn[l].reshape(1, 1, d), w_in_l, wg_l, bg_l,
                qk=qk, gw=gw, sw=sw, dk=dk, hd=hd)
            o_gla, s_new = _gla(gq.reshape(b, t, qk), gk.reshape(b, t, qk), gv.reshape(b, t, gw),
                                la.reshape(b, t, qk), gg.reshape(b, t, gw), grp["s0"](l),
                                norm_gla[l].reshape(1, gw))
            g_sb = norm_sb[l].reshape(1, sw)
            if name == "prompt":
                o_sb = _sb_prompt(sq.reshape(b, t, sw), skb.reshape(b, t, sw), svb.reshape(b, t, sw),
                                  sb_bias[l], g_sb, hd=hd)
            else:
                o_sb = _sb_sample(sq.reshape(b, t, sw), skb.reshape(b, t, sw), svb.reshape(b, t, sw),
                                  cache_k, cache_v, l, page_table, sb_bias[l], g_sb, nh=nh_sb, hd=hd)
            res = _output_projection(x, o_gla.reshape(b * t, gw), o_sb.reshape(b * t, sw), w_out_l,
                                     gate1, shift2, scale2, norm_ffn[l].reshape(1, 1, d),
                                     w_r if moe else None)
            if moe:
                x1, h2, logits = res
                x = _moe_ffn(x1, h2, logits, w_up_l, w_down_l, gate2, g_final, l == depth - 1)
            else:
                x1, h2 = res
                x = _dense_ffn(x1, h2, w_up_l, w_down_l, gate2)
            grp["x"] = x
            collected[name]["k"].append(sk.reshape(b, t, nh_sb, hd))
            collected[name]["v"].append(sv.reshape(b, t, nh_sb, hd))
            collected[name]["s"].append(s_new)

    y_prompt, y_sample = groups["prompt"]["x"], groups["sample"]["x"]
    if (depth - 1) % 2 == 0:
        y_prompt, y_sample = _final_norm(y_prompt, g_final), _final_norm(y_sample, g_final)
    cp, cs = collected["prompt"], collected["sample"]
    return (y_prompt, y_sample, jnp.stack(cp["k"]), jnp.stack(cp["v"]), jnp.stack(cp["s"]),
            jnp.stack(cs["k"]), jnp.stack(cs["v"]), jnp.stack(cs["s"]))
```

```python
import functools

import jax
import jax.numpy as jnp
from jax import lax
from jax.experimental import pallas as pl
from jax.experimental.pallas import tpu as pltpu

F32 = jnp.float32
BF16 = jnp.bfloat16
EPS = 1e-6
GATE_TAU = 16.0
GLA_CHUNK = 64
GLA_SUB = 16
TOP_K = 2
LANES = 128
SB_BLOCK = 256
SB_QUERY_BLOCKS = 4
SB_SWEEP_BLOCKS = 4
PAGES_PER_STEP = 16
LOG2E = 1.4426950408889634
SOFTPLUS2_CLAMP = 126.0
ROW_TILE = 1024
MOE_CHUNK_ROWS = 256
VMEM_LIMIT = 56 * 1024 * 1024


def _params(sem):
    return pltpu.CompilerParams(dimension_semantics=sem, vmem_limit_bytes=VMEM_LIMIT)


def _dot(a, b):
    return jnp.dot(a, b, preferred_element_type=F32)


def _dot_nt(a, b):
    return lax.dot_general(a, b, (((1,), (1,)), ((), ())), preferred_element_type=F32)


def _dot_tn(a, b):
    return lax.dot_general(a, b, (((0,), (0,)), ((), ())), preferred_element_type=F32)


def _split(x):
    hi = x.astype(BF16)
    return hi, (x - hi.astype(F32)).astype(BF16)


def _dot_hp(a, b):
    ah, al = _split(a)
    bh, bl = _split(b)
    return _dot(ah, bh) + (_dot(ah, bl) + _dot(al, bh))


def _dot_hp_exact_rhs(a, b_bf16):
    ah, al = _split(a)
    return _dot(ah, b_bf16) + _dot(al, b_bf16)


def _silu(x):
    return x * (1.0 / (1.0 + jnp.exp(-x)))


def _log_sigmoid(x):
    return jnp.minimum(x, 0.0) - jnp.log1p(jnp.exp(-jnp.abs(x)))


def _idiv(x, n):
    if n & (n - 1) == 0:
        return lax.shift_right_logical(x, n.bit_length() - 1)
    return x // n


def _iota(shape, dim):
    return lax.broadcasted_iota(jnp.int32, shape, dim)


def _modulated_norm(x, g, scale, shift):
    ms = jnp.mean(x * x, axis=-1, keepdims=True)
    return (x * lax.rsqrt(ms + EPS) * g) * (1.0 + scale) + shift


def _row_tiling(b, t, target):
    if t >= target:
        assert t % target == 0
        return 1, target
    tb = min(b, max(1, target // t))
    assert b % tb == 0
    return tb, t


def _mod_kernel(c_ref, w_ref, b_ref, o_ref):
    o_ref[0] = _dot_hp(_silu(c_ref[...]), w_ref[0]) + b_ref[0]


def _modulation(c, w_ada, b_ada):
    depth, d, n = w_ada.shape
    bc = c.shape[0]
    tn = n // 4
    return pl.pallas_call(
        _mod_kernel,
        grid=(depth, n // tn),
        in_specs=[pl.BlockSpec((bc, d), lambda l, j: (0, 0)),
                  pl.BlockSpec((1, d, tn), lambda l, j: (l, 0, j)),
                  pl.BlockSpec((1, 1, tn), lambda l, j: (l, 0, j))],
        out_specs=pl.BlockSpec((1, bc, tn), lambda l, j: (l, 0, j)),
        out_shape=jax.ShapeDtypeStruct((depth, bc, n), F32),
        compiler_params=_params(("parallel", "parallel")),
        name="adaln_modulation",
    )(c, w_ada, b_ada.reshape(depth, 1, n))


def _inproj_kernel(x_ref, shift_ref, scale_ref, g_ref, w_ref, wg_ref, bg_ref, *rest,
                   qk, gw, sw, q_scale_gla, q_scale_sb):
    gq_ref, gk_ref, gv_ref, gg_ref, la_ref, sq_ref, sk_ref, sv_ref, skb_ref, svb_ref = rest[-10:]
    tb, ts, d = x_ref.shape
    h = _modulated_norm(x_ref[...], g_ref[...], scale_ref[...], shift_ref[...])
    hb = h.reshape(tb * ts, d).astype(BF16)
    off = [0]

    def proj(width):
        a = off[0]
        off[0] = a + width
        return _dot(hb, w_ref[:, a:a + width])

    gq_ref[...] = proj(qk) * q_scale_gla
    gk_ref[...] = proj(qk)
    gv_ref[...] = proj(gw)
    gg_ref[...] = proj(gw)
    sq_ref[...] = (proj(sw) * q_scale_sb).astype(BF16)
    sk = proj(sw)
    sk_ref[...] = sk
    skb_ref[...] = sk.astype(BF16)
    sv = proj(sw)
    sv_ref[...] = sv
    svb_ref[...] = sv.astype(BF16)
    ga = proj(LANES)
    gate = _dot_hp(ga, wg_ref[...]) + bg_ref[...]
    la_ref[...] = _log_sigmoid(gate) * (1.0 / GATE_TAU)


def _input_projection(x, shift, scale, g, w, wg, bg, layer, depth, kv_prev, *, qk, gw, sw, dk, hd):
    b, t, d = x.shape
    n = b * t
    tb, ts = _row_tiling(b, t, ROW_TILE // 2)
    tm = tb * ts
    nt = t // ts
    x_spec = pl.BlockSpec((tb, ts, d), lambda i, j: (i, j, 0))
    m_spec = pl.BlockSpec((tb, 1, d), lambda i, j: (i, 0, 0))

    def full(a):
        return pl.BlockSpec(a.shape, lambda i, j: (0,) * a.ndim)

    def out(width, dtype):
        return (pl.BlockSpec((tm, width), lambda i, j: (i * nt + j, 0)),
                jax.ShapeDtypeStruct((n, width), dtype))

    def layered(width):
        return (pl.BlockSpec((None, tm, width), lambda i, j: (layer, i * nt + j, 0)),
                jax.ShapeDtypeStruct((depth, n, width), F32))

    outs = [out(qk, F32), out(qk, F32), out(gw, F32), out(gw, F32), out(qk, F32),
            out(sw, BF16), layered(sw), layered(sw), out(sw, BF16), out(sw, BF16)]
    kern = functools.partial(_inproj_kernel, qk=qk, gw=gw, sw=sw,
                             q_scale_gla=dk ** -0.5, q_scale_sb=hd ** -0.5 * LOG2E)
    in_specs = [x_spec, m_spec, m_spec, full(g), full(w), full(wg), full(bg)]
    args = [x, shift, scale, g, w, wg, bg]
    aliases = {}
    if kv_prev is not None:
        in_specs += [pl.BlockSpec(memory_space=pl.ANY)] * 2
        aliases = {len(args): 6, len(args) + 1: 7}
        args += list(kv_prev)
    return pl.pallas_call(
        kern,
        grid=(b // tb, nt),
        in_specs=in_specs,
        out_specs=[o[0] for o in outs],
        out_shape=[o[1] for o in outs],
        input_output_aliases=aliases,
        compiler_params=_params(("parallel", "parallel")),
        name="norm_input_projection",
    )(*args)


def _gla_kernel(q_ref, k_ref, v_ref, la_ref, gg_ref, s0_ref, g_ref, o_ref, s_out_ref, st_ref,
                *, chunk, sub):
    _, nh, dk, dv = s0_ref.shape
    tblock = q_ref.shape[1]
    qk, vw = nh * dk, nh * dv
    t = pl.program_id(1)

    @pl.when(t == 0)
    def _():
        rows = []
        for h in range(nh):
            rows.append(jnp.concatenate(
                [s0_ref[0, h] if hh == h else jnp.zeros((dk, dv), F32) for hh in range(nh)], axis=1))
        st_ref[...] = jnp.concatenate(rows, axis=0).T

    state_mask = _idiv(_iota((vw, qk), 0), dv) == _idiv(_iota((vw, qk), 1), dk)
    head_expand = (_idiv(_iota((qk, vw), 0), dk) == _idiv(_iota((qk, vw), 1), dv)).astype(BF16)
    key_mask = _idiv(_iota((nh * sub, qk), 0), sub) == _idiv(_iota((nh * sub, qk), 1), dk)
    val_mask = _idiv(_iota((nh * sub, vw), 0), sub) == _idiv(_iota((nh * sub, vw), 1), dv)
    tril = (_iota((chunk, chunk), 0) >= _iota((chunk, chunk), 1)).astype(BF16)
    row_id = _iota((chunk, 1), 0)
    pair_mask = _iota((sub, sub, 1), 0) >= _iota((sub, sub, 1), 1)
    n_sub = chunk // sub

    def one_chunk(ci, carry):
        r = pl.multiple_of(ci * chunk, chunk)
        q = q_ref[0, pl.ds(r, chunk), :]
        k = k_ref[0, pl.ds(r, chunk), :]
        v = v_ref[0, pl.ds(r, chunk), :]
        la_hi, la_lo = _split(la_ref[0, pl.ds(r, chunk), :])
        cum = _dot(tril, la_hi) + _dot(tril, la_lo)
        last = cum[chunk - 1:chunk, :]
        st = st_ref[...]

        o = _dot_nt((q * jnp.exp(cum)).astype(BF16), st.astype(BF16))
        k_dec = (k * jnp.exp(last - cum)).astype(BF16)
        upd = _dot_tn(v.astype(BF16), k_dec)
        st_ref[...] = st * jnp.exp(last) + jnp.where(state_mask, upd, 0.0)

        for j in range(n_sub - 1):
            a = j * sub
            ref_row = cum[a + sub - 1:a + sub, :]
            q_dec = (q * jnp.exp(jnp.minimum(cum - ref_row, 0.0))).astype(BF16)
            k_j = k[a:a + sub] * jnp.exp(ref_row - cum[a:a + sub])
            k_bd = jnp.where(key_mask, jnp.concatenate([k_j] * nh, axis=0), 0.0).astype(BF16)
            sc = _dot_nt(q_dec, k_bd)
            sc = jnp.where(row_id >= a + sub, sc, 0.0).astype(BF16)
            v_bd = jnp.where(val_mask, jnp.concatenate([v[a:a + sub]] * nh, axis=0), 0.0).astype(BF16)
            o = o + _dot(sc, v_bd)

        diag = []
        for j in range(n_sub):
            a = j * sub
            l_j, q_j, k_j, v_j = cum[a:a + sub], q[a:a + sub], k[a:a + sub], v[a:a + sub]
            dec = jnp.exp(jnp.minimum(l_j[:, None, :] - l_j[None, :, :], 0.0))
            p = jnp.where(pair_mask, (q_j[:, None, :] * k_j[None, :, :]) * dec, 0.0)
            sc = _dot(p.reshape(sub * sub, qk).astype(BF16), head_expand)
            diag.append(jnp.sum(sc.reshape(sub, sub, vw) * v_j[None, :, :], axis=1))
        o = o + (diag[0] if n_sub == 1 else jnp.concatenate(diag, axis=0))

        g = g_ref[...]
        gg = gg_ref[0, pl.ds(r, chunk), :]
        outs = []
        for h in range(nh):
            oh = o[:, h * dv:(h + 1) * dv]
            ms = jnp.mean(oh * oh, axis=-1, keepdims=True)
            outs.append(oh * lax.rsqrt(ms + EPS) * g[:, h * dv:(h + 1) * dv])
        o_ref[0, pl.ds(r, chunk), :] = jnp.concatenate(outs, axis=1) * _silu(gg)
        return carry

    lax.fori_loop(0, tblock // chunk, one_chunk, 0)

    @pl.when(t == pl.num_programs(1) - 1)
    def _():
        s_bd = st_ref[...].T
        for h in range(nh):
            s_out_ref[0, h] = s_bd[h * dk:(h + 1) * dk, h * dv:(h + 1) * dv]


def _gla(q, k, v, la, gg, s0, g):
    b, t, qk = q.shape
    vw = v.shape[-1]
    chunk = GLA_CHUNK if t % GLA_CHUNK == 0 else t
    sub = GLA_SUB if chunk % GLA_SUB == 0 else chunk
    tblock = min(t, 8 * chunk)
    assert t % tblock == 0

    def tok(width):
        return pl.BlockSpec((1, tblock, width), lambda i, j: (i, j, 0))

    s_spec = pl.BlockSpec((1,) + s0.shape[1:], lambda i, j: (i, 0, 0, 0))
    return pl.pallas_call(
        functools.partial(_gla_kernel, chunk=chunk, sub=sub),
        grid=(b, t // tblock),
        in_specs=[tok(qk), tok(qk), tok(vw), tok(qk), tok(vw), s_spec,
                  pl.BlockSpec(g.shape, lambda i, j: (0, 0))],
        out_specs=[tok(vw), s_spec],
        out_shape=[jax.ShapeDtypeStruct((b, t, vw), F32), jax.ShapeDtypeStruct(s0.shape, F32)],
        scratch_shapes=[pltpu.VMEM((vw, qk), F32)],
        compiler_params=_params(("parallel", "arbitrary")),
        name="gla_chunked",
    )(q, k, v, la, gg, s0, g)


def _softplus2(z2):
    return jnp.maximum(z2, jnp.log2(1.0 + jnp.exp2(jnp.minimum(z2, SOFTPLUS2_CLAMP))))


def _sb_within(z2, tri, valid):
    sp = _softplus2(z2)
    if valid is not None:
        sp = jnp.where(valid, sp, 0.0)
    return _dot(sp.astype(BF16), tri)


def _sb_weights(z2, within, run, valid):
    a = jnp.exp2(z2 - (within + run))
    if valid is not None:
        a = jnp.where(valid, a, 0.0)
    return a.astype(BF16), run + within[:, 0:1]


def _sb_prompt_kernel(bias_ref, q_ref, k_ref, v_ref, g_ref, o_ref, acc_ref, run_ref, *, hd, kblk):
    qblk = q_ref.shape[0]
    n_sub = qblk // kblk
    p = pl.program_id(1)
    i = pl.program_id(2)
    q = q_ref[...]
    first = _iota((1, LANES), 1) < hd
    zero = jnp.zeros_like(q)
    q_heads = (jnp.where(first, q, zero), jnp.where(first, zero, q))
    biases = (bias_ref[2 * p] * LOG2E, bias_ref[2 * p + 1] * LOG2E)
    tri = (_iota((kblk, kblk), 0) >= _iota((kblk, kblk), 1)).astype(BF16)
    acc_ref[...] = jnp.zeros_like(acc_ref)
    run_ref[...] = jnp.zeros_like(run_ref)

    def sweep(blocks, valids):
        ks = [k_ref[pl.ds(pl.multiple_of(jb * kblk, kblk), kblk), :] for jb in blocks]
        v_all = v_ref[pl.ds(pl.multiple_of(blocks[-1] * kblk, kblk), len(blocks) * kblk), :]
        for h in range(2):
            zs = [_dot_nt(q_heads[h], kb) + biases[h] for kb in ks]
            ws = [_sb_within(z, tri, valid) for z, valid in zip(zs, valids)]
            run = run_ref[h]
            parts = []
            for z, w, valid in zip(zs, ws, valids):
                a, run = _sb_weights(z, w, run, valid)
                parts.append(a)
            run_ref[h] = run
            acc_ref[h] += _dot(jnp.concatenate(parts[::-1], axis=1), v_all)

    row = _iota((qblk, kblk), 0)
    col = _iota((qblk, kblk), 1)
    diag = list(reversed(range(n_sub)))
    sweep([i * n_sub + s for s in diag], [col + s * kblk < row for s in diag])

    groups = SB_SWEEP_BLOCKS // n_sub if SB_SWEEP_BLOCKS % n_sub == 0 else 1
    per_iter = groups * n_sub
    odd = lax.rem(i, groups)
    for r in range(1, groups):
        @pl.when(odd >= r)
        def _():
            base = (i - r) * n_sub
            sweep([base + s for s in diag], [None] * n_sub)
    top = (i - odd) * n_sub
    desc = list(reversed(range(per_iter)))

    def body(j, carry):
        base = top - (j + 1) * per_iter
        sweep([base + s for s in desc], [None] * per_iter)
        return carry

    lax.fori_loop(0, i // groups, body, 0)

    o = jnp.where(first, acc_ref[0], acc_ref[1])
    ss = o * o
    s0 = jnp.sum(jnp.where(first, ss, 0.0), axis=-1, keepdims=True)
    s1 = jnp.sum(jnp.where(first, 0.0, ss), axis=-1, keepdims=True)
    ms = jnp.where(first, s0, s1) * (1.0 / hd)
    o_ref[...] = (o * lax.rsqrt(ms + EPS) * g_ref[...]).astype(BF16)


def _sb_prompt(q, k, v, bias, g, *, hd):
    b, t, w = q.shape
    kblk = SB_BLOCK if t % SB_BLOCK == 0 else t
    blk = SB_QUERY_BLOCKS * kblk if t % (SB_QUERY_BLOCKS * kblk) == 0 else kblk
    n_pairs = w // LANES
    return pl.pallas_call(
        functools.partial(_sb_prompt_kernel, hd=hd, kblk=kblk),
        grid=(b, n_pairs, t // blk),
        in_specs=[pl.BlockSpec(memory_space=pltpu.SMEM),
                  pl.BlockSpec((None, blk, LANES), lambda bi, p, i: (bi, i, p)),
                  pl.BlockSpec((None, t, LANES), lambda bi, p, i: (bi, 0, p)),
                  pl.BlockSpec((None, t, LANES), lambda bi, p, i: (bi, 0, p)),
                  pl.BlockSpec((1, LANES), lambda bi, p, i: (0, p))],
        out_specs=pl.BlockSpec((None, blk, LANES), lambda bi, p, i: (bi, i, p)),
        out_shape=jax.ShapeDtypeStruct((b, t, w), BF16),
        scratch_shapes=[pltpu.VMEM((2, blk, LANES), F32), pltpu.VMEM((2, blk, 1), F32)],
        compiler_params=_params(("parallel", "parallel", "arbitrary")),
        name="sb_attention_prompt",
    )(bias, q, k, v, g)


def _sb_sample_kernel(pt_ref, bias_ref, q_ref, kn_ref, vn_ref, g_ref, *rest, nh, hd, pages):
    k_pages, v_pages = rest[:pages], rest[pages:2 * pages]
    o_ref, acc_ref, run_ref = rest[2 * pages:]
    del pt_ref
    ts, w = q_ref.shape[1:]
    page = k_pages[0].shape[1]
    rows = nh * ts
    j = pl.program_id(1)

    row_head = _idiv(_iota((rows, w), 0), ts)
    col_head = _idiv(_iota((rows, w), 1), hd)
    q = jnp.concatenate([q_ref[0].astype(F32)] * nh, axis=0)
    q_bd = jnp.where(row_head == col_head, q, 0.0).astype(BF16)
    rh = _idiv(_iota((rows, 1), 0), ts)
    bias = jnp.zeros((rows, 1), F32)
    for h in range(nh):
        bias = jnp.where(rh == h, bias_ref[h] * LOG2E, bias)

    def tri(n):
        return (_iota((n, n), 0) >= _iota((n, n), 1)).astype(BF16)

    @pl.when(j == 0)
    def _():
        pad = jnp.zeros((LANES - ts, w), F32)
        kn = jnp.concatenate([kn_ref[0].astype(F32), pad], axis=0).astype(BF16)
        vn = jnp.concatenate([vn_ref[0].astype(F32), pad], axis=0).astype(BF16)
        z = _dot_nt(q_bd, kn) + bias
        tok = _iota((rows, LANES), 0) - _idiv(_iota((rows, LANES), 0), ts) * ts
        valid = _iota((rows, LANES), 1) < tok
        a, run = _sb_weights(z, _sb_within(z, tri(LANES), valid), jnp.zeros((rows, 1), F32), valid)
        run_ref[...] = run
        acc_ref[...] = _dot(a, vn)

    blk = 2 * page
    tri2 = tri(blk)
    kt = jnp.concatenate([r[...] for r in k_pages], axis=1).astype(BF16)
    vt = jnp.concatenate([r[...] for r in v_pages], axis=1).astype(BF16)
    z = _dot(q_bd, kt) + bias
    n_blk = pages // 2
    zs = [z[:, m * blk:(m + 1) * blk] for m in range(n_blk)]
    ws = [_sb_within(zm, tri2, None) for zm in zs]
    run = run_ref[...]
    parts = [None] * n_blk
    for m in reversed(range(n_blk)):
        parts[m], run = _sb_weights(zs[m], ws[m], run, None)
    run_ref[...] = run
    acc_ref[...] += _dot_nt(jnp.concatenate(parts, axis=1), vt)

    @pl.when(j == pl.num_programs(1) - 1)
    def _():
        acc = jnp.where(row_head == col_head, acc_ref[...], 0.0)
        o = jnp.sum(acc.reshape(nh, ts, w), axis=0)
        same_head = (_idiv(_iota((w, w), 0), hd) == _idiv(_iota((w, w), 1), hd)).astype(BF16)
        ms = _dot_hp_exact_rhs(o * o, same_head) * (1.0 / hd)
        o_ref[0] = (o * lax.rsqrt(ms + EPS) * g_ref[...]).astype(BF16)


def _sb_sample(q, k_new, v_new, cache_k, cache_v, layer, page_table, bias, g, *, nh, hd):
    b, ts, w = q.shape
    n_pages = page_table.shape[1]
    page = cache_k.shape[3]
    pages = PAGES_PER_STEP if n_pages % PAGES_PER_STEP == 0 else 2
    assert n_pages % pages == 0 and pages % 2 == 0
    steps = n_pages // pages

    def tok_spec():
        return pl.BlockSpec((1, ts, w), lambda i, j, pt: (i, 0, 0))

    def page_spec(r):
        return pl.BlockSpec((None, None, w, page),
                            lambda i, j, pt: (layer, pt[i, (steps - 1 - j) * pages + r], 0, 0))

    grid_spec = pltpu.PrefetchScalarGridSpec(
        num_scalar_prefetch=1,
        grid=(b, steps),
        in_specs=[pl.BlockSpec(memory_space=pltpu.SMEM), tok_spec(), tok_spec(), tok_spec(),
                  pl.BlockSpec(g.shape, lambda i, j, pt: (0, 0))]
                 + [page_spec(r) for r in range(pages)] * 2,
        out_specs=tok_spec(),
        scratch_shapes=[pltpu.VMEM((nh * ts, w), F32), pltpu.VMEM((nh * ts, 1), F32)],
    )
    return pl.pallas_call(
        functools.partial(_sb_sample_kernel, nh=nh, hd=hd, pages=pages),
        grid_spec=grid_spec,
        out_shape=jax.ShapeDtypeStruct((b, ts, w), BF16),
        compiler_params=_params(("parallel", "arbitrary")),
        name="sb_attention_paged",
    )(page_table, bias, q, k_new, v_new, g, *([cache_k] * pages), *([cache_v] * pages))


def _outproj_kernel(x_ref, og_ref, os_ref, w_ref, gate_ref, shift_ref, scale_ref, g_ref, *rest):
    tb, ts, d = x_ref.shape
    gw = og_ref.shape[1]
    mixed = _dot(og_ref[...].astype(BF16), w_ref[0:gw, :]) + _dot(os_ref[...], w_ref[gw:, :])
    x1 = x_ref[...] + gate_ref[...] * mixed.reshape(tb, ts, d)
    h = _modulated_norm(x1, g_ref[...], scale_ref[...], shift_ref[...]).reshape(tb * ts, d)
    if len(rest) == 4:
        wr_ref, x1_ref, h_ref, lg_ref = rest
        lg_ref[...] = _dot_hp(h, wr_ref[...])
    else:
        x1_ref, h_ref = rest
    x1_ref[...] = x1
    h_ref[...] = h.astype(BF16)


def _output_projection(x, o_gla, o_sb, w_out, gate, shift, scale, g, w_router=None):
    b, t, d = x.shape
    n = b * t
    tb, ts = _row_tiling(b, t, ROW_TILE // 2)
    tm, nt = tb * ts, t // ts
    x_spec = pl.BlockSpec((tb, ts, d), lambda i, j: (i, j, 0))
    m_spec = pl.BlockSpec((tb, 1, d), lambda i, j: (i, 0, 0))

    def rows(width):
        return pl.BlockSpec((tm, width), lambda i, j: (i * nt + j, 0))

    def full(a):
        return pl.BlockSpec(a.shape, lambda i, j: (0,) * a.ndim)

    in_specs = [x_spec, rows(o_gla.shape[1]), rows(o_sb.shape[1]), full(w_out), m_spec, m_spec, m_spec, full(g)]
    args = [x, o_gla, o_sb, w_out, gate, shift, scale, g]
    out_specs = [x_spec, rows(d)]
    out_shape = [jax.ShapeDtypeStruct((b, t, d), F32), jax.ShapeDtypeStruct((n, d), BF16)]
    if w_router is not None:
        in_specs.append(full(w_router))
        args.append(w_router)
        out_specs.append(rows(LANES))
        out_shape.append(jax.ShapeDtypeStruct((n, LANES), F32))
    return pl.pallas_call(
        _outproj_kernel,
        grid=(b // tb, nt),
        in_specs=in_specs, out_specs=out_specs, out_shape=out_shape,
        compiler_params=_params(("parallel", "parallel")),
        name="output_projection_norm",
    )(*args)


def _ffn_kernel(h_ref, wg_ref, wv_ref, wd_ref, x_ref, gate_ref, o_ref, acc_ref):
    tb, ts, d = x_ref.shape
    f = pl.program_id(2)

    @pl.when(f == 0)
    def _():
        acc_ref[...] = jnp.zeros_like(acc_ref)

    h = h_ref[...]
    act = _silu(_dot(h, wg_ref[...])) * _dot(h, wv_ref[...])
    acc_ref[...] += _dot(act.astype(BF16), wd_ref[...])

    @pl.when(f == pl.num_programs(2) - 1)
    def _():
        o_ref[...] = x_ref[...] + gate_ref[...] * acc_ref[...].reshape(tb, ts, d)


def _ffn_tile(dff, cap=512):
    for tf in range(cap - cap % LANES, 0, -LANES):
        if dff % tf == 0:
            return tf
    raise ValueError(dff)


def _dense_ffn(x, h, w_up, w_down, gate):
    b, t, d = x.shape
    dff = w_down.shape[0]
    tf = _ffn_tile(dff, cap=1536)
    nf = dff // tf
    tb, ts = _row_tiling(b, t, ROW_TILE // 2)
    tm, nt = tb * ts, t // ts
    x_spec = pl.BlockSpec((tb, ts, d), lambda i, j, f: (i, j, 0))
    return pl.pallas_call(
        _ffn_kernel,
        grid=(b // tb, nt, nf),
        in_specs=[pl.BlockSpec((tm, d), lambda i, j, f: (i * nt + j, 0)),
                  pl.BlockSpec((d, tf), lambda i, j, f: (0, f)),
                  pl.BlockSpec((d, tf), lambda i, j, f: (0, nf + f)),
                  pl.BlockSpec((tf, d), lambda i, j, f: (f, 0)),
                  x_spec,
                  pl.BlockSpec((tb, 1, d), lambda i, j, f: (i, 0, 0))],
        out_specs=x_spec,
        out_shape=jax.ShapeDtypeStruct((b, t, d), F32),
        scratch_shapes=[pltpu.VMEM((tm, d), F32)],
        compiler_params=_params(("parallel", "parallel", "arbitrary")),
        name="swiglu_ffn",
    )(h, w_up, w_up, w_down, x, gate)


def _top2_gates(logits, n_experts):
    lane = _iota(logits.shape, 1).astype(F32)
    neg = jnp.float32(-jnp.inf)
    l1 = jnp.where(lane < n_experts, logits, neg)
    m1 = jnp.max(l1, axis=-1, keepdims=True)
    i1 = jnp.min(jnp.where(l1 == m1, lane, float(LANES)), axis=-1, keepdims=True)
    l2 = jnp.where(lane == i1, neg, l1)
    m2 = jnp.max(l2, axis=-1, keepdims=True)
    i2 = jnp.min(jnp.where(l2 == m2, lane, float(LANES)), axis=-1, keepdims=True)
    p2 = jnp.exp(m2 - m1)
    w1 = 1.0 / (1.0 + p2)
    return jnp.where(lane == i1, w1, 0.0) + jnp.where(lane == i2, p2 * w1, 0.0)


def _moe_kernel(h_ref, lg_ref, wg_ref, wv_ref, wd_ref, x_ref, gate_ref, gf_ref, o_ref,
                acc_ref, gates_ref, pos_ref, pos_t_ref, xg_ref, y_ref, count_ref,
                *, n_experts, rows, final_norm):
    tb, ts, d = x_ref.shape
    tm = tb * ts
    e = pl.program_id(2)
    f = pl.program_id(3)
    last_f = pl.num_programs(3) - 1

    @pl.when((e == 0) & (f == 0))
    def _():
        acc_ref[...] = jnp.zeros_like(acc_ref)
        gates = _top2_gates(lg_ref[...], n_experts)
        gates_ref[...] = gates
        routed = gates != 0.0
        before = (_iota((tm, tm), 0) > _iota((tm, tm), 1)).astype(BF16)
        pos = jnp.where(routed, _dot(before, jnp.where(routed, 1.0, 0.0).astype(BF16)), -1.0)
        pos_ref[...] = pos
        pos_t = pos.T
        pos_t_ref[...] = pos_t
        for ee in range(n_experts):
            count_ref[ee] = (jnp.max(pos_t[ee:ee + 1, :]) + 1.0).astype(jnp.int32)

    pos_row = pos_t_ref[pl.ds(e, 1), :]
    n_chunks = _idiv(count_ref[e] + (rows - 1), rows)

    @pl.when(f == 0)
    def _():
        def gather(c, carry):
            r0 = pl.multiple_of(c * rows, rows)
            slot = (r0 + _iota((rows, 1), 0)).astype(F32)
            sel = jnp.where(pos_row == slot, 1.0, 0.0).astype(BF16)
            xg_ref[pl.ds(r0, rows), :] = _dot(sel, h_ref[...]).astype(BF16)
            y_ref[pl.ds(r0, rows), :] = jnp.zeros((rows, d), F32)
            return carry

        lax.fori_loop(0, n_chunks, gather, 0)

    def expert(c, carry):
        r0 = pl.multiple_of(c * rows, rows)
        xc = xg_ref[pl.ds(r0, rows), :]
        act = _silu(_dot(xc, wg_ref[0])) * _dot(xc, wv_ref[0])
        y_ref[pl.ds(r0, rows), :] += _dot(act.astype(BF16), wd_ref[0])
        return carry

    lax.fori_loop(0, n_chunks, expert, 0)

    @pl.when(f == last_f)
    def _():
        lane = _iota((tm, LANES), 1)
        pick = lane == e
        ge = jnp.sum(jnp.where(pick, gates_ref[...], 0.0), axis=-1, keepdims=True)
        pos_col = jnp.sum(jnp.where(pick, pos_ref[...], 0.0), axis=-1, keepdims=True)

        def scatter(c, carry):
            r0 = pl.multiple_of(c * rows, rows)
            slot = (r0 + _iota((1, rows), 1)).astype(F32)
            sel_t = jnp.where(pos_col == slot, 1.0, 0.0).astype(BF16)
            acc_ref[...] += ge * _dot(sel_t, y_ref[pl.ds(r0, rows), :].astype(BF16))
            return carry

        lax.fori_loop(0, n_chunks, scatter, 0)

    @pl.when((e == n_experts - 1) & (f == last_f))
    def _():
        x2 = x_ref[...] + gate_ref[...] * acc_ref[...].reshape(tb, ts, d)
        if final_norm:
            ms = jnp.mean(x2 * x2, axis=-1, keepdims=True)
            x2 = x2 * lax.rsqrt(ms + EPS) * gf_ref[...]
        o_ref[...] = x2


def _moe_ffn(x, h, logits, w_up, w_down, gate, g_final, final_norm):
    b, t, d = x.shape
    n_experts, dff = w_down.shape[:2]
    tf = _ffn_tile(dff, cap=1024)
    nf = dff // tf
    tb, ts = _row_tiling(b, t, ROW_TILE)
    tm, nt = tb * ts, t // ts
    x_spec = pl.BlockSpec((tb, ts, d), lambda i, j, e, f: (i, j, 0))
    rows = min(MOE_CHUNK_ROWS, tm)
    buf_rows = -(-tm // rows) * rows
    return pl.pallas_call(
        functools.partial(_moe_kernel, n_experts=n_experts, rows=rows, final_norm=final_norm),
        grid=(b // tb, nt, n_experts, nf),
        in_specs=[pl.BlockSpec((tm, d), lambda i, j, e, f: (i * nt + j, 0)),
                  pl.BlockSpec((tm, LANES), lambda i, j, e, f: (i * nt + j, 0)),
                  pl.BlockSpec((1, d, tf), lambda i, j, e, f: (e, 0, f)),
                  pl.BlockSpec((1, d, tf), lambda i, j, e, f: (e, 0, nf + f)),
                  pl.BlockSpec((1, tf, d), lambda i, j, e, f: (e, f, 0)),
                  x_spec,
                  pl.BlockSpec((tb, 1, d), lambda i, j, e, f: (i, 0, 0)),
                  pl.BlockSpec((1, 1, d), lambda i, j, e, f: (0, 0, 0))],
        out_specs=x_spec,
        out_shape=jax.ShapeDtypeStruct((b, t, d), F32),
        scratch_shapes=[pltpu.VMEM((tm, d), F32), pltpu.VMEM((tm, LANES), F32),
                        pltpu.VMEM((tm, LANES), F32), pltpu.VMEM((LANES, tm), F32),
                        pltpu.VMEM((buf_rows, d), BF16), pltpu.VMEM((buf_rows, d), F32),
                        pltpu.SMEM((n_experts,), jnp.int32)],
        compiler_params=_params(("parallel", "parallel", "arbitrary", "arbitrary")),
        name="moe_swiglu_ffn",
    )(h, logits, w_up, w_up, w_down, x, gate, g_final)


def _final_norm_kernel(x_ref, g_ref, o_ref):
    x = x_ref[...]
    ms = jnp.mean(x * x, axis=-1, keepdims=True)
    o_ref[...] = x * lax.rsqrt(ms + EPS) * g_ref[...]


def _final_norm(x, g):
    b, t, d = x.shape
    tb, ts = _row_tiling(b, t, ROW_TILE)
    x_spec = pl.BlockSpec((tb, ts, d), lambda i, j: (i, j, 0))
    return pl.pallas_call(
        _final_norm_kernel,
        grid=(b // tb, t // ts),
        in_specs=[x_spec, pl.BlockSpec(g.shape, lambda i, j: (0, 0, 0))],
        out_specs=x_spec,
        out_shape=jax.ShapeDtypeStruct((b, t, d), F32),
        compiler_params=_params(("parallel", "parallel")),
        name="final_rmsnorm",
    )(x, g)


def _arrange_w_in(w, widths):
    pieces, a = [], 0
    for wd in widths:
        pieces.append(w[:, a:a + wd])
        a += wd
    rank = pieces.pop(4)
    pieces.append(jnp.pad(rank, ((0, 0), (0, LANES - rank.shape[1]))))
    return jnp.concatenate(pieces, axis=1).astype(BF16)


def kernel(x_prompt, x_sample, c_prompt, c_sample, cache_sb_k, cache_sb_v, state_gla, page_table, w_ada, b_ada, norm_attn, norm_ffn, w_in, w_gate_up, b_gate, norm_gla, sb_bias, norm_sb, w_out, w_ff_up, w_ff_down, w_router, w_exp_up, w_exp_down, norm_final):
    depth, d = norm_attn.shape
    _, db, nh_gla, dk, dv = state_gla.shape
    _, n_phys, page, nh_sb, hd = cache_sb_k.shape
    rank, qk = w_gate_up.shape[1:]
    gw, sw = nh_gla * dv, nh_sb * hd
    widths = (qk, qk, gw, gw, rank, sw, sw, sw)
    bp, tp, _ = x_prompt.shape
    _, tsm, _ = x_sample.shape
    n_experts = w_router.shape[-1]

    mod = _modulation(jnp.concatenate([c_prompt, c_sample], axis=0), w_ada, b_ada)
    cache_k = cache_sb_k.transpose(0, 1, 3, 4, 2).reshape(depth, n_phys, sw, page)
    cache_v = cache_sb_v.transpose(0, 1, 3, 4, 2).reshape(depth, n_phys, sw, page)
    gla_zero = jnp.zeros((bp, nh_gla, dk, dv), F32)
    g_final = norm_final.reshape(1, 1, d)

    groups = {
        "prompt": dict(x=x_prompt, rows=slice(0, bp), s0=lambda l: gla_zero),
        "sample": dict(x=x_sample, rows=slice(bp, bp + db), s0=lambda l: state_gla[l]),
    }
    collected = {name: dict(s=[]) for name in groups}

    for l in range(depth):
        w_in_l = _arrange_w_in(w_in[l], widths)
        wg_l = jnp.pad(w_gate_up[l], ((0, LANES - rank), (0, 0)))
        bg_l = b_gate[l].reshape(1, qk)
        w_out_l = w_out[l].astype(BF16)
        moe = l % 2 == 1
        if moe:
            w_r = jnp.pad(w_router[l // 2], ((0, 0), (0, LANES - n_experts)))
            w_up_l = w_exp_up[l // 2].astype(BF16)
            w_down_l = w_exp_down[l // 2].astype(BF16)
        else:
            w_up_l = w_ff_up[l // 2].astype(BF16)
            w_down_l = w_ff_down[l // 2].astype(BF16)
        for name, grp in groups.items():
            x = grp["x"]
            b, t, _ = x.shape
            m = [mod[l, grp["rows"], i * d:(i + 1) * d].reshape(b, 1, d) for i in range(6)]
            shift1, scale1, gate1, shift2, scale2, gate2 = m
            gq, gk, gv, gg, la, sq, sk, sv, skb, svb = _input_projection(
                x, shift1, scale1, norm_attn[l].reshape(1, 1, d), w_in_l, wg_l, bg_l,
                l, depth, grp.get("kv"), qk=qk, gw=gw, sw=sw, dk=dk, hd=hd)
            grp["kv"] = (sk, sv)
            o_gla, s_new = _gla(gq.reshape(b, t, qk), gk.reshape(b, t, qk), gv.reshape(b, t, gw),
                                la.reshape(b, t, qk), gg.reshape(b, t, gw), grp["s0"](l),
                                norm_gla[l].reshape(1, gw))
            g_sb = norm_sb[l].reshape(1, sw)
            if name == "prompt":
                o_sb = _sb_prompt(sq.reshape(b, t, sw), skb.reshape(b, t, sw), svb.reshape(b, t, sw),
                                  sb_bias[l], g_sb, hd=hd)
            else:
                o_sb = _sb_sample(sq.reshape(b, t, sw), skb.reshape(b, t, sw), svb.reshape(b, t, sw),
                                  cache_k, cache_v, l, page_table, sb_bias[l], g_sb, nh=nh_sb, hd=hd)
            res = _output_projection(x, o_gla.reshape(b * t, gw), o_sb.reshape(b * t, sw), w_out_l,
                                     gate1, shift2, scale2, norm_ffn[l].reshape(1, 1, d),
                                     w_r if moe else None)
            if moe:
                x1, h2, logits = res
                x = _moe_ffn(x1, h2, logits, w_up_l, w_down_l, gate2, g_final, l == depth - 1)
            else:
                x1, h2 = res
                x = _dense_ffn(x1, h2, w_up_l, w_down_l, gate2)
            grp["x"] = x
            collected[name]["s"].append(s_new)

    y_prompt, y_sample = groups["prompt"]["x"], groups["sample"]["x"]
    if (depth - 1) % 2 == 0:
        y_prompt, y_sample = _final_norm(y_prompt, g_final), _final_norm(y_sample, g_final)
    def kv_out(grp):
        b, t, _ = grp["x"].shape
        return tuple(a.reshape(depth, b, t, nh_sb, hd) for a in grp["kv"])

    kp, vp = kv_out(groups["prompt"])
    ks, vs = kv_out(groups["sample"])
    return (y_prompt, y_sample, kp, vp, jnp.stack(collected["prompt"]["s"]),
            ks, vs, jnp.stack(collected["sample"]["s"]))
```

```python
import functools

import jax
import jax.numpy as jnp
from jax import lax
from jax.experimental import pallas as pl
from jax.experimental.pallas import tpu as pltpu

F32 = jnp.float32
BF16 = jnp.bfloat16
EPS = 1e-6
GATE_TAU = 16.0
GLA_CHUNK = 64
GLA_SUB = 16
TOP_K = 2
LANES = 128
SB_BLOCK = 256
SB_QUERY_BLOCKS = 4
SB_SWEEP_BLOCKS = 4
PAGES_PER_STEP = 32
LOG2E = 1.4426950408889634
SOFTPLUS2_CLAMP = 126.0
ROW_TILE = 1024
MOE_CHUNK_ROWS = 256
VMEM_LIMIT = 56 * 1024 * 1024


def _params(sem):
    return pltpu.CompilerParams(dimension_semantics=sem, vmem_limit_bytes=VMEM_LIMIT)


def _dot(a, b):
    return jnp.dot(a, b, preferred_element_type=F32)


def _dot_nt(a, b):
    return lax.dot_general(a, b, (((1,), (1,)), ((), ())), preferred_element_type=F32)


def _dot_tn(a, b):
    return lax.dot_general(a, b, (((0,), (0,)), ((), ())), preferred_element_type=F32)


def _split(x):
    hi = x.astype(BF16)
    return hi, (x - hi.astype(F32)).astype(BF16)


def _dot_hp(a, b):
    ah, al = _split(a)
    bh, bl = _split(b)
    return _dot(ah, bh) + (_dot(ah, bl) + _dot(al, bh))


def _dot_hp_exact_rhs(a, b_bf16):
    ah, al = _split(a)
    return _dot(ah, b_bf16) + _dot(al, b_bf16)


def _silu(x):
    return x * (1.0 / (1.0 + jnp.exp(-x)))


def _log_sigmoid(x):
    return jnp.minimum(x, 0.0) - jnp.log1p(jnp.exp(-jnp.abs(x)))


def _idiv(x, n):
    if n & (n - 1) == 0:
        return lax.shift_right_logical(x, n.bit_length() - 1)
    return x // n


def _iota(shape, dim):
    return lax.broadcasted_iota(jnp.int32, shape, dim)


def _modulated_norm(x, g, scale, shift):
    ms = jnp.mean(x * x, axis=-1, keepdims=True)
    return (x * lax.rsqrt(ms + EPS) * g) * (1.0 + scale) + shift


def _row_tiling(b, t, target):
    if t >= target:
        assert t % target == 0
        return 1, target
    tb = min(b, max(1, target // t))
    assert b % tb == 0
    return tb, t


def _mod_kernel(c_ref, w_ref, b_ref, o_ref):
    o_ref[0] = _dot_hp(_silu(c_ref[...]), w_ref[0]) + b_ref[0]


def _modulation(c, w_ada, b_ada):
    depth, d, n = w_ada.shape
    bc = c.shape[0]
    tn = n // 4
    return pl.pallas_call(
        _mod_kernel,
        grid=(depth, n // tn),
        in_specs=[pl.BlockSpec((bc, d), lambda l, j: (0, 0)),
                  pl.BlockSpec((1, d, tn), lambda l, j: (l, 0, j)),
                  pl.BlockSpec((1, 1, tn), lambda l, j: (l, 0, j))],
        out_specs=pl.BlockSpec((1, bc, tn), lambda l, j: (l, 0, j)),
        out_shape=jax.ShapeDtypeStruct((depth, bc, n), F32),
        compiler_params=_params(("parallel", "parallel")),
        name="adaln_modulation",
    )(c, w_ada, b_ada.reshape(depth, 1, n))


def _inproj_kernel(x_ref, shift_ref, scale_ref, g_ref, w_ref, wg_ref, bg_ref, *rest,
                   qk, gw, sw, q_scale_gla, q_scale_sb):
    gq_ref, gk_ref, gv_ref, gg_ref, la_ref, sq_ref, sk_ref, sv_ref, skb_ref, svb_ref = rest[-10:]
    tb, ts, d = x_ref.shape
    h = _modulated_norm(x_ref[...], g_ref[...], scale_ref[...], shift_ref[...])
    hb = h.reshape(tb * ts, d).astype(BF16)
    off = [0]

    def proj(width):
        a = off[0]
        off[0] = a + width
        return _dot(hb, w_ref[:, a:a + width])

    gq_ref[...] = proj(qk) * q_scale_gla
    gk_ref[...] = proj(qk)
    gv_ref[...] = proj(gw)
    gg_ref[...] = proj(gw)
    sq_ref[...] = (proj(sw) * q_scale_sb).astype(BF16)
    sk = proj(sw)
    sk_ref[...] = sk
    skb_ref[...] = sk.astype(BF16)
    sv = proj(sw)
    sv_ref[...] = sv
    svb_ref[...] = sv.astype(BF16)
    ga = proj(LANES)
    gate = _dot_hp(ga, wg_ref[...]) + bg_ref[...]
    la_ref[...] = _log_sigmoid(gate) * (1.0 / GATE_TAU)


def _input_projection(x, shift, scale, g, w, wg, bg, layer, depth, kv_prev, *, qk, gw, sw, dk, hd):
    b, t, d = x.shape
    n = b * t
    tb, ts = _row_tiling(b, t, ROW_TILE // 2)
    tm = tb * ts
    nt = t // ts
    x_spec = pl.BlockSpec((tb, ts, d), lambda i, j: (i, j, 0))
    m_spec = pl.BlockSpec((tb, 1, d), lambda i, j: (i, 0, 0))

    def full(a):
        return pl.BlockSpec(a.shape, lambda i, j: (0,) * a.ndim)

    def out(width, dtype):
        return (pl.BlockSpec((tm, width), lambda i, j: (i * nt + j, 0)),
                jax.ShapeDtypeStruct((n, width), dtype))

    def layered(width):
        return (pl.BlockSpec((None, tm, width), lambda i, j: (layer, i * nt + j, 0)),
                jax.ShapeDtypeStruct((depth, n, width), F32))

    outs = [out(qk, F32), out(qk, F32), out(gw, F32), out(gw, F32), out(qk, F32),
            out(sw, BF16), layered(sw), layered(sw), out(sw, BF16), out(sw, BF16)]
    kern = functools.partial(_inproj_kernel, qk=qk, gw=gw, sw=sw,
                             q_scale_gla=dk ** -0.5, q_scale_sb=hd ** -0.5 * LOG2E)
    in_specs = [x_spec, m_spec, m_spec, full(g), full(w), full(wg), full(bg)]
    args = [x, shift, scale, g, w, wg, bg]
    aliases = {}
    if kv_prev is not None:
        in_specs += [pl.BlockSpec(memory_space=pl.ANY)] * 2
        aliases = {len(args): 6, len(args) + 1: 7}
        args += list(kv_prev)
    return pl.pallas_call(
        kern,
        grid=(b // tb, nt),
        in_specs=in_specs,
        out_specs=[o[0] for o in outs],
        out_shape=[o[1] for o in outs],
        input_output_aliases=aliases,
        compiler_params=_params(("parallel", "parallel")),
        name="norm_input_projection",
    )(*args)


def _gla_kernel(q_ref, k_ref, v_ref, la_ref, gg_ref, s0_ref, g_ref, o_ref, s_out_ref, st_ref,
                *, chunk, sub):
    _, nh, dk, dv = s0_ref.shape
    tblock = q_ref.shape[1]
    qk, vw = nh * dk, nh * dv
    t = pl.program_id(1)

    @pl.when(t == 0)
    def _():
        rows = []
        for h in range(nh):
            rows.append(jnp.concatenate(
                [s0_ref[0, h] if hh == h else jnp.zeros((dk, dv), F32) for hh in range(nh)], axis=1))
        st_ref[...] = jnp.concatenate(rows, axis=0).T

    state_mask = _idiv(_iota((vw, qk), 0), dv) == _idiv(_iota((vw, qk), 1), dk)
    head_expand = (_idiv(_iota((qk, vw), 0), dk) == _idiv(_iota((qk, vw), 1), dv)).astype(BF16)
    key_mask = _idiv(_iota((nh * sub, qk), 0), sub) == _idiv(_iota((nh * sub, qk), 1), dk)
    val_mask = _idiv(_iota((nh * sub, vw), 0), sub) == _idiv(_iota((nh * sub, vw), 1), dv)
    tril = (_iota((chunk, chunk), 0) >= _iota((chunk, chunk), 1)).astype(BF16)
    row_id = _iota((chunk, 1), 0)
    pair_mask = _iota((sub, sub, 1), 0) >= _iota((sub, sub, 1), 1)
    n_sub = chunk // sub

    def one_chunk(ci, carry):
        r = pl.multiple_of(ci * chunk, chunk)
        q = q_ref[0, pl.ds(r, chunk), :]
        k = k_ref[0, pl.ds(r, chunk), :]
        v = v_ref[0, pl.ds(r, chunk), :]
        la_hi, la_lo = _split(la_ref[0, pl.ds(r, chunk), :])
        cum = _dot(tril, la_hi) + _dot(tril, la_lo)
        last = cum[chunk - 1:chunk, :]
        st = st_ref[...]

        o = _dot_nt((q * jnp.exp(cum)).astype(BF16), st.astype(BF16))
        k_dec = (k * jnp.exp(last - cum)).astype(BF16)
        upd = _dot_tn(v.astype(BF16), k_dec)
        st_ref[...] = st * jnp.exp(last) + jnp.where(state_mask, upd, 0.0)

        for j in range(n_sub - 1):
            a = j * sub
            ref_row = cum[a + sub - 1:a + sub, :]
            q_dec = (q * jnp.exp(jnp.minimum(cum - ref_row, 0.0))).astype(BF16)
            k_j = k[a:a + sub] * jnp.exp(ref_row - cum[a:a + sub])
            k_bd = jnp.where(key_mask, jnp.concatenate([k_j] * nh, axis=0), 0.0).astype(BF16)
            sc = _dot_nt(q_dec, k_bd)
            sc = jnp.where(row_id >= a + sub, sc, 0.0).astype(BF16)
            v_bd = jnp.where(val_mask, jnp.concatenate([v[a:a + sub]] * nh, axis=0), 0.0).astype(BF16)
            o = o + _dot(sc, v_bd)

        diag = []
        for j in range(n_sub):
            a = j * sub
            l_j, q_j, k_j, v_j = cum[a:a + sub], q[a:a + sub], k[a:a + sub], v[a:a + sub]
            dec = jnp.exp(jnp.minimum(l_j[:, None, :] - l_j[None, :, :], 0.0))
            p = jnp.where(pair_mask, (q_j[:, None, :] * k_j[None, :, :]) * dec, 0.0)
            sc = _dot(p.reshape(sub * sub, qk).astype(BF16), head_expand)
            diag.append(jnp.sum(sc.reshape(sub, sub, vw) * v_j[None, :, :], axis=1))
        o = o + (diag[0] if n_sub == 1 else jnp.concatenate(diag, axis=0))

        g = g_ref[...]
        gg = gg_ref[0, pl.ds(r, chunk), :]
        outs = []
        for h in range(nh):
            oh = o[:, h * dv:(h + 1) * dv]
            ms = jnp.mean(oh * oh, axis=-1, keepdims=True)
            outs.append(oh * lax.rsqrt(ms + EPS) * g[:, h * dv:(h + 1) * dv])
        o_ref[0, pl.ds(r, chunk), :] = jnp.concatenate(outs, axis=1) * _silu(gg)
        return carry

    lax.fori_loop(0, tblock // chunk, one_chunk, 0)

    @pl.when(t == pl.num_programs(1) - 1)
    def _():
        s_bd = st_ref[...].T
        for h in range(nh):
            s_out_ref[0, h] = s_bd[h * dk:(h + 1) * dk, h * dv:(h + 1) * dv]


def _gla(q, k, v, la, gg, s0, g):
    b, t, qk = q.shape
    vw = v.shape[-1]
    chunk = GLA_CHUNK if t % GLA_CHUNK == 0 else t
    sub = GLA_SUB if chunk % GLA_SUB == 0 else chunk
    tblock = min(t, 8 * chunk)
    assert t % tblock == 0

    def tok(width):
        return pl.BlockSpec((1, tblock, width), lambda i, j: (i, j, 0))

    s_spec = pl.BlockSpec((1,) + s0.shape[1:], lambda i, j: (i, 0, 0, 0))
    return pl.pallas_call(
        functools.partial(_gla_kernel, chunk=chunk, sub=sub),
        grid=(b, t // tblock),
        in_specs=[tok(qk), tok(qk), tok(vw), tok(qk), tok(vw), s_spec,
                  pl.BlockSpec(g.shape, lambda i, j: (0, 0))],
        out_specs=[tok(vw), s_spec],
        out_shape=[jax.ShapeDtypeStruct((b, t, vw), F32), jax.ShapeDtypeStruct(s0.shape, F32)],
        scratch_shapes=[pltpu.VMEM((vw, qk), F32)],
        compiler_params=_params(("parallel", "arbitrary")),
        name="gla_chunked",
    )(q, k, v, la, gg, s0, g)


def _softplus2(z2):
    return jnp.maximum(z2, jnp.log2(1.0 + jnp.exp2(jnp.minimum(z2, SOFTPLUS2_CLAMP))))


def _sb_within(z2, tri, valid):
    sp = _softplus2(z2)
    if valid is not None:
        sp = jnp.where(valid, sp, 0.0)
    return _dot(sp.astype(BF16), tri)


def _sb_weights(z2, within, run, valid):
    a = jnp.exp2(z2 - (within + run))
    if valid is not None:
        a = jnp.where(valid, a, 0.0)
    return a.astype(BF16), run + within[:, 0:1]


def _sb_prompt_kernel(bias_ref, q_ref, k_ref, v_ref, g_ref, o_ref, acc_ref, run_ref, *, hd, kblk):
    qblk = q_ref.shape[0]
    n_sub = qblk // kblk
    p = pl.program_id(1)
    i = pl.program_id(2)
    q = q_ref[...]
    first = _iota((1, LANES), 1) < hd
    zero = jnp.zeros_like(q)
    q_heads = (jnp.where(first, q, zero), jnp.where(first, zero, q))
    biases = (bias_ref[2 * p] * LOG2E, bias_ref[2 * p + 1] * LOG2E)
    tri = (_iota((kblk, kblk), 0) >= _iota((kblk, kblk), 1)).astype(BF16)
    acc_ref[...] = jnp.zeros_like(acc_ref)
    run_ref[...] = jnp.zeros_like(run_ref)

    def sweep(blocks, valids, r0=0, nr=qblk):
        ks = [k_ref[pl.ds(pl.multiple_of(jb * kblk, kblk), kblk), :] for jb in blocks]
        v_all = v_ref[pl.ds(pl.multiple_of(blocks[-1] * kblk, kblk), len(blocks) * kblk), :]
        for h in range(2):
            qh = q_heads[h][r0:r0 + nr]
            zs = [_dot_nt(qh, kb) + biases[h] for kb in ks]
            ws = [_sb_within(z, tri, valid) for z, valid in zip(zs, valids)]
            run = run_ref[h, r0:r0 + nr]
            parts = []
            for z, w, valid in zip(zs, ws, valids):
                a, run = _sb_weights(z, w, run, valid)
                parts.append(a)
            run_ref[h, r0:r0 + nr] = run
            acc_ref[h, r0:r0 + nr] += _dot(jnp.concatenate(parts[::-1], axis=1), v_all)

    diag = list(reversed(range(n_sub)))
    unit = min(2, n_sub)
    band = unit * kblk
    row = _iota((band, kblk), 0)
    col = _iota((band, kblk), 1)
    own = list(reversed(range(unit)))
    for u in range(n_sub // unit):
        base = i * n_sub + u * unit
        sweep([base + s for s in own], [col + s * kblk < row for s in own], u * band, band)
        for g in reversed(range(u)):
            sweep([i * n_sub + g * unit + s for s in own], [None] * unit, u * band, band)

    groups = SB_SWEEP_BLOCKS // n_sub if SB_SWEEP_BLOCKS % n_sub == 0 else 1
    per_iter = groups * n_sub
    odd = lax.rem(i, groups)
    for r in range(1, groups):
        @pl.when(odd >= r)
        def _():
            base = (i - r) * n_sub
            sweep([base + s for s in diag], [None] * n_sub)
    top = (i - odd) * n_sub
    desc = list(reversed(range(per_iter)))

    def body(j, carry):
        base = top - (j + 1) * per_iter
        sweep([base + s for s in desc], [None] * per_iter)
        return carry

    lax.fori_loop(0, i // groups, body, 0)

    o = jnp.where(first, acc_ref[0], acc_ref[1])
    ss = o * o
    s0 = jnp.sum(jnp.where(first, ss, 0.0), axis=-1, keepdims=True)
    s1 = jnp.sum(jnp.where(first, 0.0, ss), axis=-1, keepdims=True)
    ms = jnp.where(first, s0, s1) * (1.0 / hd)
    o_ref[...] = (o * lax.rsqrt(ms + EPS) * g_ref[...]).astype(BF16)


def _sb_prompt(q, k, v, bias, g, *, hd):
    b, t, w = q.shape
    kblk = SB_BLOCK if t % SB_BLOCK == 0 else t
    blk = SB_QUERY_BLOCKS * kblk if t % (SB_QUERY_BLOCKS * kblk) == 0 else kblk
    n_pairs = w // LANES
    return pl.pallas_call(
        functools.partial(_sb_prompt_kernel, hd=hd, kblk=kblk),
        grid=(b, n_pairs, t // blk),
        in_specs=[pl.BlockSpec(memory_space=pltpu.SMEM),
                  pl.BlockSpec((None, blk, LANES), lambda bi, p, i: (bi, i, p)),
                  pl.BlockSpec((None, t, LANES), lambda bi, p, i: (bi, 0, p)),
                  pl.BlockSpec((None, t, LANES), lambda bi, p, i: (bi, 0, p)),
                  pl.BlockSpec((1, LANES), lambda bi, p, i: (0, p))],
        out_specs=pl.BlockSpec((None, blk, LANES), lambda bi, p, i: (bi, i, p)),
        out_shape=jax.ShapeDtypeStruct((b, t, w), BF16),
        scratch_shapes=[pltpu.VMEM((2, blk, LANES), F32), pltpu.VMEM((2, blk, 1), F32)],
        compiler_params=_params(("parallel", "parallel", "arbitrary")),
        name="sb_attention_prompt",
    )(bias, q, k, v, g)


def _sb_sample_kernel(pt_ref, bias_ref, q_ref, kn_ref, vn_ref, g_ref, *rest, nh, hd, pages):
    k_pages, v_pages = rest[:pages], rest[pages:2 * pages]
    o_ref, acc_ref, run_ref = rest[2 * pages:]
    del pt_ref
    ts, w = q_ref.shape[1:]
    page = k_pages[0].shape[1]
    rows = nh * ts
    j = pl.program_id(1)

    row_head = _idiv(_iota((rows, w), 0), ts)
    col_head = _idiv(_iota((rows, w), 1), hd)
    q = jnp.concatenate([q_ref[0].astype(F32)] * nh, axis=0)
    q_bd = jnp.where(row_head == col_head, q, 0.0).astype(BF16)
    rh = _idiv(_iota((rows, 1), 0), ts)
    bias = jnp.zeros((rows, 1), F32)
    for h in range(nh):
        bias = jnp.where(rh == h, bias_ref[h] * LOG2E, bias)

    def tri(n):
        return (_iota((n, n), 0) >= _iota((n, n), 1)).astype(BF16)

    @pl.when(j == 0)
    def _():
        pad = jnp.zeros((LANES - ts, w), F32)
        kn = jnp.concatenate([kn_ref[0].astype(F32), pad], axis=0).astype(BF16)
        vn = jnp.concatenate([vn_ref[0].astype(F32), pad], axis=0).astype(BF16)
        z = _dot_nt(q_bd, kn) + bias
        tok = _iota((rows, LANES), 0) - _idiv(_iota((rows, LANES), 0), ts) * ts
        valid = _iota((rows, LANES), 1) < tok
        a, run = _sb_weights(z, _sb_within(z, tri(LANES), valid), jnp.zeros((rows, 1), F32), valid)
        run_ref[...] = run
        acc_ref[...] = _dot(a, vn)

    blk = 2 * page
    tri2 = tri(blk)
    kt = jnp.concatenate([r[...] for r in k_pages], axis=1).astype(BF16)
    vt = jnp.concatenate([r[...] for r in v_pages], axis=1).astype(BF16)
    z = _dot(q_bd, kt) + bias
    n_blk = pages // 2
    zs = [z[:, m * blk:(m + 1) * blk] for m in range(n_blk)]
    ws = [_sb_within(zm, tri2, None) for zm in zs]
    run = run_ref[...]
    parts = [None] * n_blk
    for m in reversed(range(n_blk)):
        parts[m], run = _sb_weights(zs[m], ws[m], run, None)
    run_ref[...] = run
    acc_ref[...] += _dot_nt(jnp.concatenate(parts, axis=1), vt)

    @pl.when(j == pl.num_programs(1) - 1)
    def _():
        acc = jnp.where(row_head == col_head, acc_ref[...], 0.0)
        o = jnp.sum(acc.reshape(nh, ts, w), axis=0)
        same_head = (_idiv(_iota((w, w), 0), hd) == _idiv(_iota((w, w), 1), hd)).astype(BF16)
        ms = _dot_hp_exact_rhs(o * o, same_head) * (1.0 / hd)
        o_ref[0] = (o * lax.rsqrt(ms + EPS) * g_ref[...]).astype(BF16)


def _sb_sample(q, k_new, v_new, cache_k, cache_v, layer, page_table, bias, g, *, nh, hd):
    b, ts, w = q.shape
    n_pages = page_table.shape[1]
    page = cache_k.shape[3]
    pages = PAGES_PER_STEP if n_pages % PAGES_PER_STEP == 0 else 2
    assert n_pages % pages == 0 and pages % 2 == 0
    steps = n_pages // pages

    def tok_spec():
        return pl.BlockSpec((1, ts, w), lambda i, j, pt: (i, 0, 0))

    def page_spec(r):
        return pl.BlockSpec((None, None, w, page),
                            lambda i, j, pt: (layer, pt[i, (steps - 1 - j) * pages + r], 0, 0))

    grid_spec = pltpu.PrefetchScalarGridSpec(
        num_scalar_prefetch=1,
        grid=(b, steps),
        in_specs=[pl.BlockSpec(memory_space=pltpu.SMEM), tok_spec(), tok_spec(), tok_spec(),
                  pl.BlockSpec(g.shape, lambda i, j, pt: (0, 0))]
                 + [page_spec(r) for r in range(pages)] * 2,
        out_specs=tok_spec(),
        scratch_shapes=[pltpu.VMEM((nh * ts, w), F32), pltpu.VMEM((nh * ts, 1), F32)],
    )
    return pl.pallas_call(
        functools.partial(_sb_sample_kernel, nh=nh, hd=hd, pages=pages),
        grid_spec=grid_spec,
        out_shape=jax.ShapeDtypeStruct((b, ts, w), BF16),
        compiler_params=_params(("parallel", "arbitrary")),
        name="sb_attention_paged",
    )(page_table, bias, q, k_new, v_new, g, *([cache_k] * pages), *([cache_v] * pages))


def _outproj_kernel(x_ref, og_ref, os_ref, w_ref, gate_ref, shift_ref, scale_ref, g_ref, *rest):
    tb, ts, d = x_ref.shape
    gw = og_ref.shape[1]
    mixed = _dot(og_ref[...].astype(BF16), w_ref[0:gw, :]) + _dot(os_ref[...], w_ref[gw:, :])
    x1 = x_ref[...] + gate_ref[...] * mixed.reshape(tb, ts, d)
    h = _modulated_norm(x1, g_ref[...], scale_ref[...], shift_ref[...]).reshape(tb * ts, d)
    if len(rest) == 4:
        wr_ref, x1_ref, h_ref, lg_ref = rest
        lg_ref[...] = _dot_hp(h, wr_ref[...])
    else:
        x1_ref, h_ref = rest
    x1_ref[...] = x1
    h_ref[...] = h.astype(BF16)


def _output_projection(x, o_gla, o_sb, w_out, gate, shift, scale, g, w_router=None):
    b, t, d = x.shape
    n = b * t
    tb, ts = _row_tiling(b, t, ROW_TILE // 2)
    tm, nt = tb * ts, t // ts
    x_spec = pl.BlockSpec((tb, ts, d), lambda i, j: (i, j, 0))
    m_spec = pl.BlockSpec((tb, 1, d), lambda i, j: (i, 0, 0))

    def rows(width):
        return pl.BlockSpec((tm, width), lambda i, j: (i * nt + j, 0))

    def full(a):
        return pl.BlockSpec(a.shape, lambda i, j: (0,) * a.ndim)

    in_specs = [x_spec, rows(o_gla.shape[1]), rows(o_sb.shape[1]), full(w_out), m_spec, m_spec, m_spec, full(g)]
    args = [x, o_gla, o_sb, w_out, gate, shift, scale, g]
    out_specs = [x_spec, rows(d)]
    out_shape = [jax.ShapeDtypeStruct((b, t, d), F32), jax.ShapeDtypeStruct((n, d), BF16)]
    if w_router is not None:
        in_specs.append(full(w_router))
        args.append(w_router)
        out_specs.append(rows(LANES))
        out_shape.append(jax.ShapeDtypeStruct((n, LANES), F32))
    return pl.pallas_call(
        _outproj_kernel,
        grid=(b // tb, nt),
        in_specs=in_specs, out_specs=out_specs, out_shape=out_shape,
        compiler_params=_params(("parallel", "parallel")),
        name="output_projection_norm",
    )(*args)


def _ffn_kernel(h_ref, wg_ref, wv_ref, wd_ref, x_ref, gate_ref, o_ref, acc_ref):
    tb, ts, d = x_ref.shape
    f = pl.program_id(2)

    @pl.when(f == 0)
    def _():
        acc_ref[...] = jnp.zeros_like(acc_ref)

    h = h_ref[...]
    act = _silu(_dot(h, wg_ref[...])) * _dot(h, wv_ref[...])
    acc_ref[...] += _dot(act.astype(BF16), wd_ref[...])

    @pl.when(f == pl.num_programs(2) - 1)
    def _():
        o_ref[...] = x_ref[...] + gate_ref[...] * acc_ref[...].reshape(tb, ts, d)


def _ffn_tile(dff, cap=512):
    for tf in range(cap - cap % LANES, 0, -LANES):
        if dff % tf == 0:
            return tf
    raise ValueError(dff)


def _dense_ffn(x, h, w_up, w_down, gate):
    b, t, d = x.shape
    dff = w_down.shape[0]
    tf = _ffn_tile(dff, cap=1536)
    nf = dff // tf
    tb, ts = _row_tiling(b, t, ROW_TILE // 2)
    tm, nt = tb * ts, t // ts
    x_spec = pl.BlockSpec((tb, ts, d), lambda i, j, f: (i, j, 0))
    return pl.pallas_call(
        _ffn_kernel,
        grid=(b // tb, nt, nf),
        in_specs=[pl.BlockSpec((tm, d), lambda i, j, f: (i * nt + j, 0)),
                  pl.BlockSpec((d, tf), lambda i, j, f: (0, f)),
                  pl.BlockSpec((d, tf), lambda i, j, f: (0, nf + f)),
                  pl.BlockSpec((tf, d), lambda i, j, f: (f, 0)),
                  x_spec,
                  pl.BlockSpec((tb, 1, d), lambda i, j, f: (i, 0, 0))],
        out_specs=x_spec,
        out_shape=jax.ShapeDtypeStruct((b, t, d), F32),
        scratch_shapes=[pltpu.VMEM((tm, d), F32)],
        compiler_params=_params(("parallel", "parallel", "arbitrary")),
        name="swiglu_ffn",
    )(h, w_up, w_up, w_down, x, gate)


def _top2_gates(logits, n_experts):
    lane = _iota(logits.shape, 1).astype(F32)
    neg = jnp.float32(-jnp.inf)
    l1 = jnp.where(lane < n_experts, logits, neg)
    m1 = jnp.max(l1, axis=-1, keepdims=True)
    i1 = jnp.min(jnp.where(l1 == m1, lane, float(LANES)), axis=-1, keepdims=True)
    l2 = jnp.where(lane == i1, neg, l1)
    m2 = jnp.max(l2, axis=-1, keepdims=True)
    i2 = jnp.min(jnp.where(l2 == m2, lane, float(LANES)), axis=-1, keepdims=True)
    p2 = jnp.exp(m2 - m1)
    w1 = 1.0 / (1.0 + p2)
    return jnp.where(lane == i1, w1, 0.0) + jnp.where(lane == i2, p2 * w1, 0.0)


def _moe_kernel(h_ref, lg_ref, wg_ref, wv_ref, wd_ref, x_ref, gate_ref, gf_ref, o_ref,
                acc_ref, gates_ref, pos_ref, pos_t_ref, xg_ref, y_ref, count_ref,
                *, n_experts, rows, final_norm):
    tb, ts, d = x_ref.shape
    tm = tb * ts
    e = pl.program_id(2)
    f = pl.program_id(3)
    last_f = pl.num_programs(3) - 1

    @pl.when((e == 0) & (f == 0))
    def _():
        acc_ref[...] = jnp.zeros_like(acc_ref)
        gates = _top2_gates(lg_ref[...], n_experts)
        gates_ref[...] = gates
        routed = gates != 0.0
        before = (_iota((tm, tm), 0) > _iota((tm, tm), 1)).astype(BF16)
        pos = jnp.where(routed, _dot(before, jnp.where(routed, 1.0, 0.0).astype(BF16)), -1.0)
        pos_ref[...] = pos
        pos_t = pos.T
        pos_t_ref[...] = pos_t
        for ee in range(n_experts):
            count_ref[ee] = (jnp.max(pos_t[ee:ee + 1, :]) + 1.0).astype(jnp.int32)

    pos_row = pos_t_ref[pl.ds(e, 1), :]
    n_chunks = _idiv(count_ref[e] + (rows - 1), rows)

    @pl.when(f == 0)
    def _():
        def gather(c, carry):
            r0 = pl.multiple_of(c * rows, rows)
            slot = (r0 + _iota((rows, 1), 0)).astype(F32)
            sel = jnp.where(pos_row == slot, 1.0, 0.0).astype(BF16)
            xg_ref[pl.ds(r0, rows), :] = _dot(sel, h_ref[...]).astype(BF16)
            y_ref[pl.ds(r0, rows), :] = jnp.zeros((rows, d), F32)
            return carry

        lax.fori_loop(0, n_chunks, gather, 0)

    def expert(c, carry):
        r0 = pl.multiple_of(c * rows, rows)
        xc = xg_ref[pl.ds(r0, rows), :]
        act = _silu(_dot(xc, wg_ref[0])) * _dot(xc, wv_ref[0])
        y_ref[pl.ds(r0, rows), :] += _dot(act.astype(BF16), wd_ref[0])
        return carry

    lax.fori_loop(0, n_chunks, expert, 0)

    @pl.when(f == last_f)
    def _():
        lane = _iota((tm, LANES), 1)
        pick = lane == e
        ge = jnp.sum(jnp.where(pick, gates_ref[...], 0.0), axis=-1, keepdims=True)
        pos_col = jnp.sum(jnp.where(pick, pos_ref[...], 0.0), axis=-1, keepdims=True)

        def scatter(c, carry):
            r0 = pl.multiple_of(c * rows, rows)
            slot = (r0 + _iota((1, rows), 1)).astype(F32)
            sel_t = jnp.where(pos_col == slot, 1.0, 0.0).astype(BF16)
            acc_ref[...] += ge * _dot(sel_t, y_ref[pl.ds(r0, rows), :].astype(BF16))
            return carry

        lax.fori_loop(0, n_chunks, scatter, 0)

    @pl.when((e == n_experts - 1) & (f == last_f))
    def _():
        x2 = x_ref[...] + gate_ref[...] * acc_ref[...].reshape(tb, ts, d)
        if final_norm:
            ms = jnp.mean(x2 * x2, axis=-1, keepdims=True)
            x2 = x2 * lax.rsqrt(ms + EPS) * gf_ref[...]
        o_ref[...] = x2


def _moe_ffn(x, h, logits, w_up, w_down, gate, g_final, final_norm):
    b, t, d = x.shape
    n_experts, dff = w_down.shape[:2]
    tf = _ffn_tile(dff, cap=1024)
    nf = dff // tf
    tb, ts = _row_tiling(b, t, ROW_TILE)
    tm, nt = tb * ts, t // ts
    x_spec = pl.BlockSpec((tb, ts, d), lambda i, j, e, f: (i, j, 0))
    rows = min(MOE_CHUNK_ROWS, tm)
    buf_rows = -(-tm // rows) * rows
    return pl.pallas_call(
        functools.partial(_moe_kernel, n_experts=n_experts, rows=rows, final_norm=final_norm),
        grid=(b // tb, nt, n_experts, nf),
        in_specs=[pl.BlockSpec((tm, d), lambda i, j, e, f: (i * nt + j, 0)),
                  pl.BlockSpec((tm, LANES), lambda i, j, e, f: (i * nt + j, 0)),
                  pl.BlockSpec((1, d, tf), lambda i, j, e, f: (e, 0, f)),
                  pl.BlockSpec((1, d, tf), lambda i, j, e, f: (e, 0, nf + f)),
                  pl.BlockSpec((1, tf, d), lambda i, j, e, f: (e, f, 0)),
                  x_spec,
                  pl.BlockSpec((tb, 1, d), lambda i, j, e, f: (i, 0, 0)),
                  pl.BlockSpec((1, 1, d), lambda i, j, e, f: (0, 0, 0))],
        out_specs=x_spec,
        out_shape=jax.ShapeDtypeStruct((b, t, d), F32),
        scratch_shapes=[pltpu.VMEM((tm, d), F32), pltpu.VMEM((tm, LANES), F32),
                        pltpu.VMEM((tm, LANES), F32), pltpu.VMEM((LANES, tm), F32),
                        pltpu.VMEM((buf_rows, d), BF16), pltpu.VMEM((buf_rows, d), F32),
                        pltpu.SMEM((n_experts,), jnp.int32)],
        compiler_params=_params(("parallel", "parallel", "arbitrary", "arbitrary")),
        name="moe_swiglu_ffn",
    )(h, logits, w_up, w_up, w_down, x, gate, g_final)


def _final_norm_kernel(x_ref, g_ref, o_ref):
    x = x_ref[...]
    ms = jnp.mean(x * x, axis=-1, keepdims=True)
    o_ref[...] = x * lax.rsqrt(ms + EPS) * g_ref[...]


def _final_norm(x, g):
    b, t, d = x.shape
    tb, ts = _row_tiling(b, t, ROW_TILE)
    x_spec = pl.BlockSpec((tb, ts, d), lambda i, j: (i, j, 0))
    return pl.pallas_call(
        _final_norm_kernel,
        grid=(b // tb, t // ts),
        in_specs=[x_spec, pl.BlockSpec(g.shape, lambda i, j: (0, 0, 0))],
        out_specs=x_spec,
        out_shape=jax.ShapeDtypeStruct((b, t, d), F32),
        compiler_params=_params(("parallel", "parallel")),
        name="final_rmsnorm",
    )(x, g)


def _arrange_w_in(w, widths):
    pieces, a = [], 0
    for wd in widths:
        pieces.append(w[:, a:a + wd])
        a += wd
    rank = pieces.pop(4)
    pieces.append(jnp.pad(rank, ((0, 0), (0, LANES - rank.shape[1]))))
    return jnp.concatenate(pieces, axis=1).astype(BF16)


def kernel(x_prompt, x_sample, c_prompt, c_sample, cache_sb_k, cache_sb_v, state_gla, page_table, w_ada, b_ada, norm_attn, norm_ffn, w_in, w_gate_up, b_gate, norm_gla, sb_bias, norm_sb, w_out, w_ff_up, w_ff_down, w_router, w_exp_up, w_exp_down, norm_final):
    depth, d = norm_attn.shape
    _, db, nh_gla, dk, dv = state_gla.shape
    _, n_phys, page, nh_sb, hd = cache_sb_k.shape
    rank, qk = w_gate_up.shape[1:]
    gw, sw = nh_gla * dv, nh_sb * hd
    widths = (qk, qk, gw, gw, rank, sw, sw, sw)
    bp, tp, _ = x_prompt.shape
    _, tsm, _ = x_sample.shape
    n_experts = w_router.shape[-1]

    mod = _modulation(jnp.concatenate([c_prompt, c_sample], axis=0), w_ada, b_ada)
    cache_k = cache_sb_k.transpose(0, 1, 3, 4, 2).reshape(depth, n_phys, sw, page)
    cache_v = cache_sb_v.transpose(0, 1, 3, 4, 2).reshape(depth, n_phys, sw, page)
    gla_zero = jnp.zeros((bp, nh_gla, dk, dv), F32)
    g_final = norm_final.reshape(1, 1, d)

    groups = {
        "prompt": dict(x=x_prompt, rows=slice(0, bp), s0=lambda l: gla_zero),
        "sample": dict(x=x_sample, rows=slice(bp, bp + db), s0=lambda l: state_gla[l]),
    }
    collected = {name: dict(s=[]) for name in groups}

    for l in range(depth):
        w_in_l = _arrange_w_in(w_in[l], widths)
        wg_l = jnp.pad(w_gate_up[l], ((0, LANES - rank), (0, 0)))
        bg_l = b_gate[l].reshape(1, qk)
        w_out_l = w_out[l].astype(BF16)
        moe = l % 2 == 1
        if moe:
            w_r = jnp.pad(w_router[l // 2], ((0, 0), (0, LANES - n_experts)))
            w_up_l = w_exp_up[l // 2].astype(BF16)
            w_down_l = w_exp_down[l // 2].astype(BF16)
        else:
            w_up_l = w_ff_up[l // 2].astype(BF16)
            w_down_l = w_ff_down[l // 2].astype(BF16)
        for name, grp in groups.items():
            x = grp["x"]
            b, t, _ = x.shape
            m = [mod[l, grp["rows"], i * d:(i + 1) * d].reshape(b, 1, d) for i in range(6)]
            shift1, scale1, gate1, shift2, scale2, gate2 = m
            gq, gk, gv, gg, la, sq, sk, sv, skb, svb = _input_projection(
                x, shift1, scale1, norm_attn[l].reshape(1, 1, d), w_in_l, wg_l, bg_l,
                l, depth, grp.get("kv"), qk=qk, gw=gw, sw=sw, dk=dk, hd=hd)
            grp["kv"] = (sk, sv)
            o_gla, s_new = _gla(gq.reshape(b, t, qk), gk.reshape(b, t, qk), gv.reshape(b, t, gw),
                                la.reshape(b, t, qk), gg.reshape(b, t, gw), grp["s0"](l),
                                norm_gla[l].reshape(1, gw))
            g_sb = norm_sb[l].reshape(1, sw)
            if name == "prompt":
                o_sb = _sb_prompt(sq.reshape(b, t, sw), skb.reshape(b, t, sw), svb.reshape(b, t, sw),
                                  sb_bias[l], g_sb, hd=hd)
            else:
                o_sb = _sb_sample(sq.reshape(b, t, sw), skb.reshape(b, t, sw), svb.reshape(b, t, sw),
                                  cache_k, cache_v, l, page_table, sb_bias[l], g_sb, nh=nh_sb, hd=hd)
            res = _output_projection(x, o_gla.reshape(b * t, gw), o_sb.reshape(b * t, sw), w_out_l,
                                     gate1, shift2, scale2, norm_ffn[l].reshape(1, 1, d),
                                     w_r if moe else None)
            if moe:
                x1, h2, logits = res
                x = _moe_ffn(x1, h2, logits, w_up_l, w_down_l, gate2, g_final, l == depth - 1)
            else:
                x1, h2 = res
                x = _dense_ffn(x1, h2, w_up_l, w_down_l, gate2)
            grp["x"] = x
            collected[name]["s"].append(s_new)

    y_prompt, y_sample = groups["prompt"]["x"], groups["sample"]["x"]
    if (depth - 1) % 2 == 0:
        y_prompt, y_sample = _final_norm(y_prompt, g_final), _final_norm(y_sample, g_final)
    def kv_out(grp):
        b, t, _ = grp["x"].shape
        return tuple(a.reshape(depth, b, t, nh_sb, hd) for a in grp["kv"])

    kp, vp = kv_out(groups["prompt"])
    ks, vs = kv_out(groups["sample"])
    return (y_prompt, y_sample, kp, vp, jnp.stack(collected["prompt"]["s"]),
            ks, vs, jnp.stack(collected["sample"]["s"]))
```

```python
import functools

import jax
import jax.numpy as jnp
from jax import lax
from jax.experimental import pallas as pl
from jax.experimental.pallas import tpu as pltpu

F32 = jnp.float32
BF16 = jnp.bfloat16
EPS = 1e-6
GATE_TAU = 16.0
GLA_CHUNK = 64
GLA_SUB = 16
GLA_UNROLL = 4
TOP_K = 2
LANES = 128
SB_BLOCK = 256
SB_QUERY_BLOCKS = 4
SB_SWEEP_BLOCKS = 4
PAGES_PER_STEP = 32
LOG2E = 1.4426950408889634
SOFTPLUS2_CLAMP = 126.0
ROW_TILE = 1024
MOE_CHUNK_ROWS = 256
VMEM_LIMIT = 56 * 1024 * 1024


def _params(sem):
    return pltpu.CompilerParams(dimension_semantics=sem, vmem_limit_bytes=VMEM_LIMIT)


def _dot(a, b):
    return jnp.dot(a, b, preferred_element_type=F32)


def _dot_nt(a, b):
    return lax.dot_general(a, b, (((1,), (1,)), ((), ())), preferred_element_type=F32)


def _dot_tn(a, b):
    return lax.dot_general(a, b, (((0,), (0,)), ((), ())), preferred_element_type=F32)


def _split(x):
    hi = x.astype(BF16)
    return hi, (x - hi.astype(F32)).astype(BF16)


def _dot_hp(a, b):
    ah, al = _split(a)
    bh, bl = _split(b)
    return _dot(ah, bh) + (_dot(ah, bl) + _dot(al, bh))


def _dot_hp_exact_rhs(a, b_bf16):
    ah, al = _split(a)
    return _dot(ah, b_bf16) + _dot(al, b_bf16)


def _silu(x):
    return x * (1.0 / (1.0 + jnp.exp(-x)))


def _log_sigmoid(x):
    return jnp.minimum(x, 0.0) - jnp.log1p(jnp.exp(-jnp.abs(x)))


def _idiv(x, n):
    if n & (n - 1) == 0:
        return lax.shift_right_logical(x, n.bit_length() - 1)
    return x // n


def _iota(shape, dim):
    return lax.broadcasted_iota(jnp.int32, shape, dim)


def _modulated_norm(x, g, scale, shift):
    ms = jnp.mean(x * x, axis=-1, keepdims=True)
    return (x * lax.rsqrt(ms + EPS) * g) * (1.0 + scale) + shift


def _row_tiling(b, t, target):
    if t >= target:
        assert t % target == 0
        return 1, target
    tb = min(b, max(1, target // t))
    assert b % tb == 0
    return tb, t


def _mod_kernel(c_ref, w_ref, b_ref, o_ref):
    o_ref[0] = _dot_hp(_silu(c_ref[...]), w_ref[0]) + b_ref[0]


def _modulation(c, w_ada, b_ada):
    depth, d, n = w_ada.shape
    bc = c.shape[0]
    tn = n // 4
    return pl.pallas_call(
        _mod_kernel,
        grid=(depth, n // tn),
        in_specs=[pl.BlockSpec((bc, d), lambda l, j: (0, 0)),
                  pl.BlockSpec((1, d, tn), lambda l, j: (l, 0, j)),
                  pl.BlockSpec((1, 1, tn), lambda l, j: (l, 0, j))],
        out_specs=pl.BlockSpec((1, bc, tn), lambda l, j: (l, 0, j)),
        out_shape=jax.ShapeDtypeStruct((depth, bc, n), F32),
        compiler_params=_params(("parallel", "parallel")),
        name="adaln_modulation",
    )(c, w_ada, b_ada.reshape(depth, 1, n))


def _inproj_kernel(x_ref, shift_ref, scale_ref, g_ref, w_ref, wg_ref, bg_ref, *rest,
                   qk, gw, sw, q_scale_gla, q_scale_sb):
    gq_ref, gk_ref, gv_ref, gg_ref, la_ref, sq_ref, sk_ref, sv_ref, skb_ref, svb_ref = rest[-10:]
    tb, ts, d = x_ref.shape
    h = _modulated_norm(x_ref[...], g_ref[...], scale_ref[...], shift_ref[...])
    hb = h.reshape(tb * ts, d).astype(BF16)
    off = [0]

    def proj(width):
        a = off[0]
        off[0] = a + width
        return _dot(hb, w_ref[:, a:a + width])

    gq_ref[...] = proj(qk) * q_scale_gla
    gk_ref[...] = proj(qk)
    gv_ref[...] = proj(gw)
    gg_ref[...] = proj(gw)
    sq_ref[...] = (proj(sw) * q_scale_sb).astype(BF16)
    sk = proj(sw)
    sk_ref[...] = sk
    skb_ref[...] = sk.astype(BF16)
    sv = proj(sw)
    sv_ref[...] = sv
    svb_ref[...] = sv.astype(BF16)
    ga = proj(LANES)
    gate = _dot_hp(ga, wg_ref[...]) + bg_ref[...]
    la_ref[...] = _log_sigmoid(gate) * (1.0 / GATE_TAU)


def _input_projection(x, shift, scale, g, w, wg, bg, layer, depth, kv_prev, *, qk, gw, sw, dk, hd):
    b, t, d = x.shape
    n = b * t
    tb, ts = _row_tiling(b, t, ROW_TILE // 2)
    tm = tb * ts
    nt = t // ts
    x_spec = pl.BlockSpec((tb, ts, d), lambda i, j: (i, j, 0))
    m_spec = pl.BlockSpec((tb, 1, d), lambda i, j: (i, 0, 0))

    def full(a):
        return pl.BlockSpec(a.shape, lambda i, j: (0,) * a.ndim)

    def out(width, dtype):
        return (pl.BlockSpec((tm, width), lambda i, j: (i * nt + j, 0)),
                jax.ShapeDtypeStruct((n, width), dtype))

    def layered(width):
        return (pl.BlockSpec((None, tm, width), lambda i, j: (layer, i * nt + j, 0)),
                jax.ShapeDtypeStruct((depth, n, width), F32))

    outs = [out(qk, F32), out(qk, F32), out(gw, F32), out(gw, F32), out(qk, F32),
            out(sw, BF16), layered(sw), layered(sw), out(sw, BF16), out(sw, BF16)]
    kern = functools.partial(_inproj_kernel, qk=qk, gw=gw, sw=sw,
                             q_scale_gla=dk ** -0.5, q_scale_sb=hd ** -0.5 * LOG2E)
    in_specs = [x_spec, m_spec, m_spec, full(g), full(w), full(wg), full(bg)]
    args = [x, shift, scale, g, w, wg, bg]
    aliases = {}
    if kv_prev is not None:
        in_specs += [pl.BlockSpec(memory_space=pl.ANY)] * 2
        aliases = {len(args): 6, len(args) + 1: 7}
        args += list(kv_prev)
    return pl.pallas_call(
        kern,
        grid=(b // tb, nt),
        in_specs=in_specs,
        out_specs=[o[0] for o in outs],
        out_shape=[o[1] for o in outs],
        input_output_aliases=aliases,
        compiler_params=_params(("parallel", "parallel")),
        name="norm_input_projection",
    )(*args)


def _gla_kernel(q_ref, k_ref, v_ref, la_ref, gg_ref, s0_ref, g_ref, o_ref, s_out_ref, st_ref,
                *, chunk, sub):
    _, nh, dk, dv = s0_ref.shape
    tblock = q_ref.shape[1]
    qk, vw = nh * dk, nh * dv
    t = pl.program_id(1)

    @pl.when(t == 0)
    def _():
        rows = []
        for h in range(nh):
            rows.append(jnp.concatenate(
                [s0_ref[0, h] if hh == h else jnp.zeros((dk, dv), F32) for hh in range(nh)], axis=1))
        st_ref[...] = jnp.concatenate(rows, axis=0).T

    state_mask = _idiv(_iota((vw, qk), 0), dv) == _idiv(_iota((vw, qk), 1), dk)
    head_expand = (_idiv(_iota((qk, vw), 0), dk) == _idiv(_iota((qk, vw), 1), dv)).astype(BF16)
    key_mask = _idiv(_iota((nh * sub, qk), 0), sub) == _idiv(_iota((nh * sub, qk), 1), dk)
    val_mask = _idiv(_iota((nh * sub, vw), 0), sub) == _idiv(_iota((nh * sub, vw), 1), dv)
    tril = (_iota((chunk, chunk), 0) >= _iota((chunk, chunk), 1)).astype(BF16)
    row_id = _iota((chunk, 1), 0)
    pair_mask = _iota((sub, sub, 1), 1) >= _iota((sub, sub, 1), 0)
    n_sub = chunk // sub

    def one_chunk(ci, carry):
        r = pl.multiple_of(ci * chunk, chunk)
        q = q_ref[0, pl.ds(r, chunk), :]
        k = k_ref[0, pl.ds(r, chunk), :]
        v = v_ref[0, pl.ds(r, chunk), :]
        la_hi, la_lo = _split(la_ref[0, pl.ds(r, chunk), :])
        cum = _dot(tril, la_hi) + _dot(tril, la_lo)
        last = cum[chunk - 1:chunk, :]
        st = st_ref[...]

        o = _dot_nt((q * jnp.exp(cum)).astype(BF16), st.astype(BF16))
        k_dec = (k * jnp.exp(last - cum)).astype(BF16)
        upd = _dot_tn(v.astype(BF16), k_dec)
        st_ref[...] = st * jnp.exp(last) + jnp.where(state_mask, upd, 0.0)

        for j in range(n_sub - 1):
            a = j * sub
            ref_row = cum[a + sub - 1:a + sub, :]
            q_dec = (q * jnp.exp(jnp.minimum(cum - ref_row, 0.0))).astype(BF16)
            k_j = k[a:a + sub] * jnp.exp(ref_row - cum[a:a + sub])
            k_bd = jnp.where(key_mask, jnp.concatenate([k_j] * nh, axis=0), 0.0).astype(BF16)
            sc = _dot_nt(q_dec, k_bd)
            sc = jnp.where(row_id >= a + sub, sc, 0.0).astype(BF16)
            v_bd = jnp.where(val_mask, jnp.concatenate([v[a:a + sub]] * nh, axis=0), 0.0).astype(BF16)
            o = o + _dot(sc, v_bd)

        diag = []
        for j in range(n_sub):
            a = j * sub
            l_j, q_j, k_j, v_j = cum[a:a + sub], q[a:a + sub], k[a:a + sub], v[a:a + sub]
            dec = jnp.exp(jnp.minimum(l_j[None, :, :] - l_j[:, None, :], 0.0))
            p = jnp.where(pair_mask, (q_j[None, :, :] * k_j[:, None, :]) * dec, 0.0)
            sc = _dot(p.reshape(sub * sub, qk).astype(BF16), head_expand)
            diag.append(jnp.sum(sc.reshape(sub, sub, vw) * v_j[:, None, :], axis=0))
        o = o + (diag[0] if n_sub == 1 else jnp.concatenate(diag, axis=0))

        g = g_ref[...]
        gg = gg_ref[0, pl.ds(r, chunk), :]
        outs = []
        for h in range(nh):
            oh = o[:, h * dv:(h + 1) * dv]
            ms = jnp.mean(oh * oh, axis=-1, keepdims=True)
            outs.append(oh * lax.rsqrt(ms + EPS) * g[:, h * dv:(h + 1) * dv])
        o_ref[0, pl.ds(r, chunk), :] = jnp.concatenate(outs, axis=1) * _silu(gg)
        return carry

    n_chunks = tblock // chunk
    lax.fori_loop(0, n_chunks, one_chunk, 0, unroll=GLA_UNROLL if n_chunks % GLA_UNROLL == 0 else 1)

    @pl.when(t == pl.num_programs(1) - 1)
    def _():
        s_bd = st_ref[...].T
        for h in range(nh):
            s_out_ref[0, h] = s_bd[h * dk:(h + 1) * dk, h * dv:(h + 1) * dv]


def _gla(q, k, v, la, gg, s0, g):
    b, t, qk = q.shape
    vw = v.shape[-1]
    chunk = GLA_CHUNK if t % GLA_CHUNK == 0 else t
    sub = GLA_SUB if chunk % GLA_SUB == 0 else chunk
    tblock = min(t, 8 * chunk)
    assert t % tblock == 0

    def tok(width):
        return pl.BlockSpec((1, tblock, width), lambda i, j: (i, j, 0))

    s_spec = pl.BlockSpec((1,) + s0.shape[1:], lambda i, j: (i, 0, 0, 0))
    return pl.pallas_call(
        functools.partial(_gla_kernel, chunk=chunk, sub=sub),
        grid=(b, t // tblock),
        in_specs=[tok(qk), tok(qk), tok(vw), tok(qk), tok(vw), s_spec,
                  pl.BlockSpec(g.shape, lambda i, j: (0, 0))],
        out_specs=[tok(vw), s_spec],
        out_shape=[jax.ShapeDtypeStruct((b, t, vw), F32), jax.ShapeDtypeStruct(s0.shape, F32)],
        scratch_shapes=[pltpu.VMEM((vw, qk), F32)],
        compiler_params=_params(("parallel", "arbitrary")),
        name="gla_chunked",
    )(q, k, v, la, gg, s0, g)


def _softplus2(z2):
    return jnp.maximum(z2, jnp.log2(1.0 + jnp.exp2(jnp.minimum(z2, SOFTPLUS2_CLAMP))))


def _sb_within(z2, tri, valid):
    sp = _softplus2(z2)
    if valid is not None:
        sp = jnp.where(valid, sp, 0.0)
    return _dot(sp.astype(BF16), tri)


def _sb_weights(z2, within, run, valid):
    a = jnp.exp2(z2 - (within + run))
    if valid is not None:
        a = jnp.where(valid, a, 0.0)
    return a.astype(BF16), run + within[:, 0:1]


def _sb_prompt_kernel(bias_ref, q_ref, k_ref, v_ref, g_ref, o_ref, acc_ref, run_ref, *, hd, kblk):
    qblk = q_ref.shape[0]
    n_sub = qblk // kblk
    p = pl.program_id(1)
    i = pl.program_id(2)
    q = q_ref[...]
    first = _iota((1, LANES), 1) < hd
    zero = jnp.zeros_like(q)
    q_heads = (jnp.where(first, q, zero), jnp.where(first, zero, q))
    biases = (bias_ref[2 * p] * LOG2E, bias_ref[2 * p + 1] * LOG2E)
    tri = (_iota((kblk, kblk), 0) >= _iota((kblk, kblk), 1)).astype(BF16)
    acc_ref[...] = jnp.zeros_like(acc_ref)
    run_ref[...] = jnp.zeros_like(run_ref)

    def sweep(blocks, valids, r0=0, nr=qblk):
        ks = [k_ref[pl.ds(pl.multiple_of(jb * kblk, kblk), kblk), :] for jb in blocks]
        v_all = v_ref[pl.ds(pl.multiple_of(blocks[-1] * kblk, kblk), len(blocks) * kblk), :]
        for h in range(2):
            qh = q_heads[h][r0:r0 + nr]
            zs = [_dot_nt(qh, kb) + biases[h] for kb in ks]
            ws = [_sb_within(z, tri, valid) for z, valid in zip(zs, valids)]
            run = run_ref[h, r0:r0 + nr]
            parts = []
            for z, w, valid in zip(zs, ws, valids):
                a, run = _sb_weights(z, w, run, valid)
                parts.append(a)
            run_ref[h, r0:r0 + nr] = run
            acc_ref[h, r0:r0 + nr] += _dot(jnp.concatenate(parts[::-1], axis=1), v_all)

    diag = list(reversed(range(n_sub)))
    unit = min(2, n_sub)
    band = unit * kblk
    row = _iota((band, kblk), 0)
    col = _iota((band, kblk), 1)
    own = list(reversed(range(unit)))
    for u in range(n_sub // unit):
        base = i * n_sub + u * unit
        sweep([base + s for s in own], [col + s * kblk < row for s in own], u * band, band)
        for g in reversed(range(u)):
            sweep([i * n_sub + g * unit + s for s in own], [None] * unit, u * band, band)

    groups = SB_SWEEP_BLOCKS // n_sub if SB_SWEEP_BLOCKS % n_sub == 0 else 1
    per_iter = groups * n_sub
    odd = lax.rem(i, groups)
    for r in range(1, groups):
        @pl.when(odd >= r)
        def _():
            base = (i - r) * n_sub
            sweep([base + s for s in diag], [None] * n_sub)
    top = (i - odd) * n_sub
    desc = list(reversed(range(per_iter)))

    def body(j, carry):
        base = top - (j + 1) * per_iter
        sweep([base + s for s in desc], [None] * per_iter)
        return carry

    lax.fori_loop(0, i // groups, body, 0)

    o = jnp.where(first, acc_ref[0], acc_ref[1])
    ss = o * o
    s0 = jnp.sum(jnp.where(first, ss, 0.0), axis=-1, keepdims=True)
    s1 = jnp.sum(jnp.where(first, 0.0, ss), axis=-1, keepdims=True)
    ms = jnp.where(first, s0, s1) * (1.0 / hd)
    o_ref[...] = (o * lax.rsqrt(ms + EPS) * g_ref[...]).astype(BF16)


def _sb_prompt(q, k, v, bias, g, *, hd):
    b, t, w = q.shape
    kblk = SB_BLOCK if t % SB_BLOCK == 0 else t
    blk = SB_QUERY_BLOCKS * kblk if t % (SB_QUERY_BLOCKS * kblk) == 0 else kblk
    n_pairs = w // LANES
    return pl.pallas_call(
        functools.partial(_sb_prompt_kernel, hd=hd, kblk=kblk),
        grid=(b, n_pairs, t // blk),
        in_specs=[pl.BlockSpec(memory_space=pltpu.SMEM),
                  pl.BlockSpec((None, blk, LANES), lambda bi, p, i: (bi, i, p)),
                  pl.BlockSpec((None, t, LANES), lambda bi, p, i: (bi, 0, p)),
                  pl.BlockSpec((None, t, LANES), lambda bi, p, i: (bi, 0, p)),
                  pl.BlockSpec((1, LANES), lambda bi, p, i: (0, p))],
        out_specs=pl.BlockSpec((None, blk, LANES), lambda bi, p, i: (bi, i, p)),
        out_shape=jax.ShapeDtypeStruct((b, t, w), BF16),
        scratch_shapes=[pltpu.VMEM((2, blk, LANES), F32), pltpu.VMEM((2, blk, 1), F32)],
        compiler_params=_params(("parallel", "parallel", "arbitrary")),
        name="sb_attention_prompt",
    )(bias, q, k, v, g)


def _sb_sample_kernel(pt_ref, bias_ref, q_ref, kn_ref, vn_ref, g_ref, *rest, nh, hd, pages):
    k_pages, v_pages = rest[:pages], rest[pages:2 * pages]
    o_ref, acc_ref, run_ref = rest[2 * pages:]
    del pt_ref
    ts, w = q_ref.shape[1:]
    page = k_pages[0].shape[1]
    rows = nh * ts
    j = pl.program_id(1)

    row_head = _idiv(_iota((rows, w), 0), ts)
    col_head = _idiv(_iota((rows, w), 1), hd)
    q = jnp.concatenate([q_ref[0].astype(F32)] * nh, axis=0)
    q_bd = jnp.where(row_head == col_head, q, 0.0).astype(BF16)
    rh = _idiv(_iota((rows, 1), 0), ts)
    bias = jnp.zeros((rows, 1), F32)
    for h in range(nh):
        bias = jnp.where(rh == h, bias_ref[h] * LOG2E, bias)

    def tri(n):
        return (_iota((n, n), 0) >= _iota((n, n), 1)).astype(BF16)

    @pl.when(j == 0)
    def _():
        pad = jnp.zeros((LANES - ts, w), F32)
        kn = jnp.concatenate([kn_ref[0].astype(F32), pad], axis=0).astype(BF16)
        vn = jnp.concatenate([vn_ref[0].astype(F32), pad], axis=0).astype(BF16)
        z = _dot_nt(q_bd, kn) + bias
        tok = _iota((rows, LANES), 0) - _idiv(_iota((rows, LANES), 0), ts) * ts
        valid = _iota((rows, LANES), 1) < tok
        a, run = _sb_weights(z, _sb_within(z, tri(LANES), valid), jnp.zeros((rows, 1), F32), valid)
        run_ref[...] = run
        acc_ref[...] = _dot(a, vn)

    blk = 2 * page
    tri2 = tri(blk)
    kt = jnp.concatenate([r[...] for r in k_pages], axis=1).astype(BF16)
    vt = jnp.concatenate([r[...] for r in v_pages], axis=1).astype(BF16)
    z = _dot(q_bd, kt) + bias
    n_blk = pages // 2
    zs = [z[:, m * blk:(m + 1) * blk] for m in range(n_blk)]
    ws = [_sb_within(zm, tri2, None) for zm in zs]
    run = run_ref[...]
    parts = [None] * n_blk
    for m in reversed(range(n_blk)):
        parts[m], run = _sb_weights(zs[m], ws[m], run, None)
    run_ref[...] = run
    acc_ref[...] += _dot_nt(jnp.concatenate(parts, axis=1), vt)

    @pl.when(j == pl.num_programs(1) - 1)
    def _():
        acc = jnp.where(row_head == col_head, acc_ref[...], 0.0)
        o = jnp.sum(acc.reshape(nh, ts, w), axis=0)
        same_head = (_idiv(_iota((w, w), 0), hd) == _idiv(_iota((w, w), 1), hd)).astype(BF16)
        ms = _dot_hp_exact_rhs(o * o, same_head) * (1.0 / hd)
        o_ref[0] = (o * lax.rsqrt(ms + EPS) * g_ref[...]).astype(BF16)


def _sb_sample(q, k_new, v_new, cache_k, cache_v, layer, page_table, bias, g, *, nh, hd):
    b, ts, w = q.shape
    n_pages = page_table.shape[1]
    page = cache_k.shape[3]
    pages = PAGES_PER_STEP if n_pages % PAGES_PER_STEP == 0 else 2
    assert n_pages % pages == 0 and pages % 2 == 0
    steps = n_pages // pages

    def tok_spec():
        return pl.BlockSpec((1, ts, w), lambda i, j, pt: (i, 0, 0))

    def page_spec(r):
        return pl.BlockSpec((None, None, w, page),
                            lambda i, j, pt: (layer, pt[i, (steps - 1 - j) * pages + r], 0, 0))

    grid_spec = pltpu.PrefetchScalarGridSpec(
        num_scalar_prefetch=1,
        grid=(b, steps),
        in_specs=[pl.BlockSpec(memory_space=pltpu.SMEM), tok_spec(), tok_spec(), tok_spec(),
                  pl.BlockSpec(g.shape, lambda i, j, pt: (0, 0))]
                 + [page_spec(r) for r in range(pages)] * 2,
        out_specs=tok_spec(),
        scratch_shapes=[pltpu.VMEM((nh * ts, w), F32), pltpu.VMEM((nh * ts, 1), F32)],
    )
    return pl.pallas_call(
        functools.partial(_sb_sample_kernel, nh=nh, hd=hd, pages=pages),
        grid_spec=grid_spec,
        out_shape=jax.ShapeDtypeStruct((b, ts, w), BF16),
        compiler_params=_params(("parallel", "arbitrary")),
        name="sb_attention_paged",
    )(page_table, bias, q, k_new, v_new, g, *([cache_k] * pages), *([cache_v] * pages))


def _outproj_kernel(x_ref, og_ref, os_ref, w_ref, gate_ref, shift_ref, scale_ref, g_ref, *rest):
    tb, ts, d = x_ref.shape
    gw = og_ref.shape[1]
    mixed = _dot(og_ref[...].astype(BF16), w_ref[0:gw, :]) + _dot(os_ref[...], w_ref[gw:, :])
    x1 = x_ref[...] + gate_ref[...] * mixed.reshape(tb, ts, d)
    h = _modulated_norm(x1, g_ref[...], scale_ref[...], shift_ref[...]).reshape(tb * ts, d)
    if len(rest) == 4:
        wr_ref, x1_ref, h_ref, lg_ref = rest
        lg_ref[...] = _dot_hp(h, wr_ref[...])
    else:
        x1_ref, h_ref = rest
    x1_ref[...] = x1
    h_ref[...] = h.astype(BF16)


def _output_projection(x, o_gla, o_sb, w_out, gate, shift, scale, g, w_router=None):
    b, t, d = x.shape
    n = b * t
    tb, ts = _row_tiling(b, t, ROW_TILE // 2)
    tm, nt = tb * ts, t // ts
    x_spec = pl.BlockSpec((tb, ts, d), lambda i, j: (i, j, 0))
    m_spec = pl.BlockSpec((tb, 1, d), lambda i, j: (i, 0, 0))

    def rows(width):
        return pl.BlockSpec((tm, width), lambda i, j: (i * nt + j, 0))

    def full(a):
        return pl.BlockSpec(a.shape, lambda i, j: (0,) * a.ndim)

    in_specs = [x_spec, rows(o_gla.shape[1]), rows(o_sb.shape[1]), full(w_out), m_spec, m_spec, m_spec, full(g)]
    args = [x, o_gla, o_sb, w_out, gate, shift, scale, g]
    out_specs = [x_spec, rows(d)]
    out_shape = [jax.ShapeDtypeStruct((b, t, d), F32), jax.ShapeDtypeStruct((n, d), BF16)]
    if w_router is not None:
        in_specs.append(full(w_router))
        args.append(w_router)
        out_specs.append(rows(LANES))
        out_shape.append(jax.ShapeDtypeStruct((n, LANES), F32))
    return pl.pallas_call(
        _outproj_kernel,
        grid=(b // tb, nt),
        in_specs=in_specs, out_specs=out_specs, out_shape=out_shape,
        compiler_params=_params(("parallel", "parallel")),
        name="output_projection_norm",
    )(*args)


def _ffn_kernel(h_ref, wg_ref, wv_ref, wd_ref, x_ref, gate_ref, o_ref, acc_ref):
    tb, ts, d = x_ref.shape
    f = pl.program_id(2)

    @pl.when(f == 0)
    def _():
        acc_ref[...] = jnp.zeros_like(acc_ref)

    h = h_ref[...]
    act = _silu(_dot(h, wg_ref[...])) * _dot(h, wv_ref[...])
    acc_ref[...] += _dot(act.astype(BF16), wd_ref[...])

    @pl.when(f == pl.num_programs(2) - 1)
    def _():
        o_ref[...] = x_ref[...] + gate_ref[...] * acc_ref[...].reshape(tb, ts, d)


def _ffn_tile(dff, cap=512):
    for tf in range(cap - cap % LANES, 0, -LANES):
        if dff % tf == 0:
            return tf
    raise ValueError(dff)


def _dense_ffn(x, h, w_up, w_down, gate):
    b, t, d = x.shape
    dff = w_down.shape[0]
    tf = _ffn_tile(dff, cap=1536)
    nf = dff // tf
    tb, ts = _row_tiling(b, t, ROW_TILE // 2)
    tm, nt = tb * ts, t // ts
    x_spec = pl.BlockSpec((tb, ts, d), lambda i, j, f: (i, j, 0))
    return pl.pallas_call(
        _ffn_kernel,
        grid=(b // tb, nt, nf),
        in_specs=[pl.BlockSpec((tm, d), lambda i, j, f: (i * nt + j, 0)),
                  pl.BlockSpec((d, tf), lambda i, j, f: (0, f)),
                  pl.BlockSpec((d, tf), lambda i, j, f: (0, nf + f)),
                  pl.BlockSpec((tf, d), lambda i, j, f: (f, 0)),
                  x_spec,
                  pl.BlockSpec((tb, 1, d), lambda i, j, f: (i, 0, 0))],
        out_specs=x_spec,
        out_shape=jax.ShapeDtypeStruct((b, t, d), F32),
        scratch_shapes=[pltpu.VMEM((tm, d), F32)],
        compiler_params=_params(("parallel", "parallel", "arbitrary")),
        name="swiglu_ffn",
    )(h, w_up, w_up, w_down, x, gate)


def _top2_gates(logits, n_experts):
    lane = _iota(logits.shape, 1).astype(F32)
    neg = jnp.float32(-jnp.inf)
    l1 = jnp.where(lane < n_experts, logits, neg)
    m1 = jnp.max(l1, axis=-1, keepdims=True)
    i1 = jnp.min(jnp.where(l1 == m1, lane, float(LANES)), axis=-1, keepdims=True)
    l2 = jnp.where(lane == i1, neg, l1)
    m2 = jnp.max(l2, axis=-1, keepdims=True)
    i2 = jnp.min(jnp.where(l2 == m2, lane, float(LANES)), axis=-1, keepdims=True)
    p2 = jnp.exp(m2 - m1)
    w1 = 1.0 / (1.0 + p2)
    return jnp.where(lane == i1, w1, 0.0) + jnp.where(lane == i2, p2 * w1, 0.0)


def _moe_kernel(h_ref, lg_ref, wg_ref, wv_ref, wd_ref, x_ref, gate_ref, gf_ref, o_ref,
                acc_ref, gates_ref, pos_ref, pos_t_ref, xg_ref, y_ref, count_ref,
                *, n_experts, rows, final_norm):
    tb, ts, d = x_ref.shape
    tm = tb * ts
    e = pl.program_id(2)
    f = pl.program_id(3)
    last_f = pl.num_programs(3) - 1

    @pl.when((e == 0) & (f == 0))
    def _():
        acc_ref[...] = jnp.zeros_like(acc_ref)
        gates = _top2_gates(lg_ref[...], n_experts)
        gates_ref[...] = gates
        routed = gates != 0.0
        before = (_iota((tm, tm), 0) > _iota((tm, tm), 1)).astype(BF16)
        pos = jnp.where(routed, _dot(before, jnp.where(routed, 1.0, 0.0).astype(BF16)), -1.0)
        pos_ref[...] = pos
        pos_t = pos.T
        pos_t_ref[...] = pos_t
        for ee in range(n_experts):
            count_ref[ee] = (jnp.max(pos_t[ee:ee + 1, :]) + 1.0).astype(jnp.int32)

    pos_row = pos_t_ref[pl.ds(e, 1), :]
    n_chunks = _idiv(count_ref[e] + (rows - 1), rows)

    @pl.when(f == 0)
    def _():
        def gather(c, carry):
            r0 = pl.multiple_of(c * rows, rows)
            slot = (r0 + _iota((rows, 1), 0)).astype(F32)
            sel = jnp.where(pos_row == slot, 1.0, 0.0).astype(BF16)
            xg_ref[pl.ds(r0, rows), :] = _dot(sel, h_ref[...]).astype(BF16)
            y_ref[pl.ds(r0, rows), :] = jnp.zeros((rows, d), F32)
            return carry

        lax.fori_loop(0, n_chunks, gather, 0)

    def expert(c, carry):
        r0 = pl.multiple_of(c * rows, rows)
        xc = xg_ref[pl.ds(r0, rows), :]
        act = _silu(_dot(xc, wg_ref[0])) * _dot(xc, wv_ref[0])
        y_ref[pl.ds(r0, rows), :] += _dot(act.astype(BF16), wd_ref[0])
        return carry

    lax.fori_loop(0, n_chunks, expert, 0)

    @pl.when(f == last_f)
    def _():
        lane = _iota((tm, LANES), 1)
        pick = lane == e
        ge = jnp.sum(jnp.where(pick, gates_ref[...], 0.0), axis=-1, keepdims=True)
        pos_col = jnp.sum(jnp.where(pick, pos_ref[...], 0.0), axis=-1, keepdims=True)

        def scatter(c, carry):
            r0 = pl.multiple_of(c * rows, rows)
            slot = (r0 + _iota((1, rows), 1)).astype(F32)
            sel_t = jnp.where(pos_col == slot, 1.0, 0.0).astype(BF16)
            acc_ref[...] += ge * _dot(sel_t, y_ref[pl.ds(r0, rows), :].astype(BF16))
            return carry

        lax.fori_loop(0, n_chunks, scatter, 0)

    @pl.when((e == n_experts - 1) & (f == last_f))
    def _():
        x2 = x_ref[...] + gate_ref[...] * acc_ref[...].reshape(tb, ts, d)
        if final_norm:
            ms = jnp.mean(x2 * x2, axis=-1, keepdims=True)
            x2 = x2 * lax.rsqrt(ms + EPS) * gf_ref[...]
        o_ref[...] = x2


def _moe_ffn(x, h, logits, w_up, w_down, gate, g_final, final_norm):
    b, t, d = x.shape
    n_experts, dff = w_down.shape[:2]
    tf = _ffn_tile(dff, cap=1024)
    nf = dff // tf
    tb, ts = _row_tiling(b, t, ROW_TILE)
    tm, nt = tb * ts, t // ts
    x_spec = pl.BlockSpec((tb, ts, d), lambda i, j, e, f: (i, j, 0))
    rows = min(MOE_CHUNK_ROWS, tm)
    buf_rows = -(-tm // rows) * rows
    return pl.pallas_call(
        functools.partial(_moe_kernel, n_experts=n_experts, rows=rows, final_norm=final_norm),
        grid=(b // tb, nt, n_experts, nf),
        in_specs=[pl.BlockSpec((tm, d), lambda i, j, e, f: (i * nt + j, 0)),
                  pl.BlockSpec((tm, LANES), lambda i, j, e, f: (i * nt + j, 0)),
                  pl.BlockSpec((1, d, tf), lambda i, j, e, f: (e, 0, f)),
                  pl.BlockSpec((1, d, tf), lambda i, j, e, f: (e, 0, nf + f)),
                  pl.BlockSpec((1, tf, d), lambda i, j, e, f: (e, f, 0)),
                  x_spec,
                  pl.BlockSpec((tb, 1, d), lambda i, j, e, f: (i, 0, 0)),
                  pl.BlockSpec((1, 1, d), lambda i, j, e, f: (0, 0, 0))],
        out_specs=x_spec,
        out_shape=jax.ShapeDtypeStruct((b, t, d), F32),
        scratch_shapes=[pltpu.VMEM((tm, d), F32), pltpu.VMEM((tm, LANES), F32),
                        pltpu.VMEM((tm, LANES), F32), pltpu.VMEM((LANES, tm), F32),
                        pltpu.VMEM((buf_rows, d), BF16), pltpu.VMEM((buf_rows, d), F32),
                        pltpu.SMEM((n_experts,), jnp.int32)],
        compiler_params=_params(("parallel", "parallel", "arbitrary", "arbitrary")),
        name="moe_swiglu_ffn",
    )(h, logits, w_up, w_up, w_down, x, gate, g_final)


def _final_norm_kernel(x_ref, g_ref, o_ref):
    x = x_ref[...]
    ms = jnp.mean(x * x, axis=-1, keepdims=True)
    o_ref[...] = x * lax.rsqrt(ms + EPS) * g_ref[...]


def _final_norm(x, g):
    b, t, d = x.shape
    tb, ts = _row_tiling(b, t, ROW_TILE)
    x_spec = pl.BlockSpec((tb, ts, d), lambda i, j: (i, j, 0))
    return pl.pallas_call(
        _final_norm_kernel,
        grid=(b // tb, t // ts),
        in_specs=[x_spec, pl.BlockSpec(g.shape, lambda i, j: (0, 0, 0))],
        out_specs=x_spec,
        out_shape=jax.ShapeDtypeStruct((b, t, d), F32),
        compiler_params=_params(("parallel", "parallel")),
        name="final_rmsnorm",
    )(x, g)


def _arrange_w_in(w, widths):
    pieces, a = [], 0
    for wd in widths:
        pieces.append(w[:, a:a + wd])
        a += wd
    rank = pieces.pop(4)
    pieces.append(jnp.pad(rank, ((0, 0), (0, LANES - rank.shape[1]))))
    return jnp.concatenate(pieces, axis=1).astype(BF16)


def kernel(x_prompt, x_sample, c_prompt, c_sample, cache_sb_k, cache_sb_v, state_gla, page_table, w_ada, b_ada, norm_attn, norm_ffn, w_in, w_gate_up, b_gate, norm_gla, sb_bias, norm_sb, w_out, w_ff_up, w_ff_down, w_router, w_exp_up, w_exp_down, norm_final):
    depth, d = norm_attn.shape
    _, db, nh_gla, dk, dv = state_gla.shape
    _, n_phys, page, nh_sb, hd = cache_sb_k.shape
    rank, qk = w_gate_up.shape[1:]
    gw, sw = nh_gla * dv, nh_sb * hd
    widths = (qk, qk, gw, gw, rank, sw, sw, sw)
    bp, tp, _ = x_prompt.shape
    _, tsm, _ = x_sample.shape
    n_experts = w_router.shape[-1]

    mod = _modulation(jnp.concatenate([c_prompt, c_sample], axis=0), w_ada, b_ada)
    cache_k = cache_sb_k.transpose(0, 1, 3, 4, 2).reshape(depth, n_phys, sw, page)
    cache_v = cache_sb_v.transpose(0, 1, 3, 4, 2).reshape(depth, n_phys, sw, page)
    gla_zero = jnp.zeros((bp, nh_gla, dk, dv), F32)
    g_final = norm_final.reshape(1, 1, d)

    groups = {
        "prompt": dict(x=x_prompt, rows=slice(0, bp), s0=lambda l: gla_zero),
        "sample": dict(x=x_sample, rows=slice(bp, bp + db), s0=lambda l: state_gla[l]),
    }
    collected = {name: dict(s=[]) for name in groups}

    for l in range(depth):
        w_in_l = _arrange_w_in(w_in[l], widths)
        wg_l = jnp.pad(w_gate_up[l], ((0, LANES - rank), (0, 0)))
        bg_l = b_gate[l].reshape(1, qk)
        w_out_l = w_out[l].astype(BF16)
        moe = l % 2 == 1
        if moe:
            w_r = jnp.pad(w_router[l // 2], ((0, 0), (0, LANES - n_experts)))
            w_up_l = w_exp_up[l // 2].astype(BF16)
            w_down_l = w_exp_down[l // 2].astype(BF16)
        else:
            w_up_l = w_ff_up[l // 2].astype(BF16)
            w_down_l = w_ff_down[l // 2].astype(BF16)
        for name, grp in groups.items():
            x = grp["x"]
            b, t, _ = x.shape
            m = [mod[l, grp["rows"], i * d:(i + 1) * d].reshape(b, 1, d) for i in range(6)]
            shift1, scale1, gate1, shift2, scale2, gate2 = m
            gq, gk, gv, gg, la, sq, sk, sv, skb, svb = _input_projection(
                x, shift1, scale1, norm_attn[l].reshape(1, 1, d), w_in_l, wg_l, bg_l,
                l, depth, grp.get("kv"), qk=qk, gw=gw, sw=sw, dk=dk, hd=hd)
            grp["kv"] = (sk, sv)
            o_gla, s_new = _gla(gq.reshape(b, t, qk), gk.reshape(b, t, qk), gv.reshape(b, t, gw),
                                la.reshape(b, t, qk), gg.reshape(b, t, gw), grp["s0"](l),
                                norm_gla[l].reshape(1, gw))
            g_sb = norm_sb[l].reshape(1, sw)
            if name == "prompt":
                o_sb = _sb_prompt(sq.reshape(b, t, sw), skb.reshape(b, t, sw), svb.reshape(b, t, sw),
                                  sb_bias[l], g_sb, hd=hd)
            else:
                o_sb = _sb_sample(sq.reshape(b, t, sw), skb.reshape(b, t, sw), svb.reshape(b, t, sw),
                                  cache_k, cache_v, l, page_table, sb_bias[l], g_sb, nh=nh_sb, hd=hd)
            res = _output_projection(x, o_gla.reshape(b * t, gw), o_sb.reshape(b * t, sw), w_out_l,
                                     gate1, shift2, scale2, norm_ffn[l].reshape(1, 1, d),
                                     w_r if moe else None)
            if moe:
                x1, h2, logits = res
                x = _moe_ffn(x1, h2, logits, w_up_l, w_down_l, gate2, g_final, l == depth - 1)
            else:
                x1, h2 = res
                x = _dense_ffn(x1, h2, w_up_l, w_down_l, gate2)
            grp["x"] = x
            collected[name]["s"].append(s_new)

    y_prompt, y_sample = groups["prompt"]["x"], groups["sample"]["x"]
    if (depth - 1) % 2 == 0:
        y_prompt, y_sample = _final_norm(y_prompt, g_final), _final_norm(y_sample, g_final)
    def kv_out(grp):
        b, t, _ = grp["x"].shape
        return tuple(a.reshape(depth, b, t, nh_sb, hd) for a in grp["kv"])

    kp, vp = kv_out(groups["prompt"])
    ks, vs = kv_out(groups["sample"])
    return (y_prompt, y_sample, kp, vp, jnp.stack(collected["prompt"]["s"]),
            ks, vs, jnp.stack(collected["sample"]["s"]))
```
